```python
import jax, jax.numpy as jnp
from jax import lax
import numpy as np

D_MODEL = 1024
BATCH = 4
SEQ = 4096
DEPTH = 1

CHUNK = 64
SB_HEADS = 8
SB_HEAD_DIM = 64
SB_WIDTH = SB_HEADS * SB_HEAD_DIM
Q_BLOCK = 128
S5_WIDTH = D_MODEL // 2
S5_GROUP = 16
S5_GROUPS = S5_WIDTH // S5_GROUP
S5_STATE = 64
S5_DT_MIN = 1e-3
S5_DT_MAX = 1e-1
N_EXPERTS = 32
TOP_K = 4
D_FF = D_MODEL
SWIGLU_LIMIT = 7.0
SWIGLU_ALPHA = 1.702
RMS_EPS = 1e-5
OFF_K = SB_WIDTH
OFF_V = 2 * SB_WIDTH
OFF_U = 3 * SB_WIDTH
OFF_GA = OFF_U + S5_WIDTH
OFF_GB = OFF_GA + D_MODEL
IN_WIDTH = OFF_GB + D_MODEL

kernel_name = "hybrid_stickbreak_s5_moe_block"


def rmsnorm(x, g):
    xf = x.astype(jnp.float32)
    return xf * lax.rsqrt(jnp.mean(xf * xf, axis=-1, keepdims=True) + RMS_EPS) * g.astype(jnp.float32)


def split_heads(t):
    b, l, _ = t.shape
    return t.reshape(b, l, SB_HEADS, SB_HEAD_DIM).transpose(0, 2, 1, 3)


def stick_breaking_attention(q, k, v):
    scale = SB_HEAD_DIM ** -0.5
    seq = q.shape[2]
    outs = []
    for start in range(0, seq, Q_BLOCK):
        end = start + Q_BLOCK
        qb = q[:, :, start:end]
        kb = k[:, :, :end]
        vb = v[:, :, :end]
        z = jnp.einsum('bhqd,bhkd->bhqk', qb, kb).astype(jnp.float32) * scale
        before = jnp.arange(end)[None, :] < jnp.arange(start, end)[:, None]
        log_keep = jnp.where(before, jax.nn.log_sigmoid(-z), 0.0)
        later = lax.cumsum(log_keep, axis=3, reverse=True) - log_keep
        w = jnp.where(before, jnp.exp(jax.nn.log_sigmoid(z) + later), 0.0)
        outs.append(jnp.einsum('bhqk,bhkd->bhqd', w, vb.astype(jnp.float32)))
    return jnp.concatenate(outs, axis=2)


def s5_mixer(u, lam_re, lam_im, b_re, b_im, c_re, c_im, d_skip, log_step, w_glu):
    f32 = jnp.float32
    bsz, seq, _ = u.shape
    uf = u.astype(f32).reshape(bsz, seq, S5_GROUPS, S5_GROUP)
    lam = lax.complex(lam_re.astype(f32), lam_im.astype(f32))
    step = jnp.exp(log_step.astype(f32))[:, None]
    lam_bar = jnp.exp(lam * step)
    b_mat = lax.complex(b_re.astype(f32), b_im.astype(f32))
    b_bar = ((lam_bar - 1.0) / lam)[:, :, None] * b_mat
    bu = jnp.einsum('gph,blgh->blgp', b_bar, uf)
    decay = jnp.broadcast_to(lam_bar, (1, seq, S5_GROUPS, S5_STATE))

    def combine(left, right):
        a_l, s_l = left
        a_r, s_r = right
        return a_l * a_r, a_r * s_l + s_r

    _, states = lax.associative_scan(combine, (decay, bu), axis=1)
    c_mat = lax.complex(c_re.astype(f32), c_im.astype(f32))
    y = jnp.real(jnp.einsum('ghp,blgp->blgh', c_mat, states)) + d_skip.astype(f32) * uf
    y = jax.nn.gelu(y.reshape(bsz, seq, S5_WIDTH))
    return y * jax.nn.sigmoid(y @ w_glu)


def moe_ffn(h, w_router, b_router, w_up, b_up, w_down, b_down):
    bsz, seq, d = h.shape
    t = h.reshape(bsz * seq, d)
    logits = (t @ w_router + b_router).astype(jnp.float32)
    top_val, top_idx = lax.top_k(logits, TOP_K)
    top_w = jax.nn.softmax(top_val, axis=-1)
    gates = jnp.sum(jax.nn.one_hot(top_idx, N_EXPERTS, dtype=jnp.float32) * top_w[..., None], axis=1)
    out = jnp.zeros((bsz * seq, d), jnp.float32)
    for e in range(N_EXPERTS):
        hid = t @ w_up[e] + b_up[e]
        glu = jnp.minimum(hid[:, :D_FF], SWIGLU_LIMIT)
        lin = jnp.clip(hid[:, D_FF:], -SWIGLU_LIMIT, SWIGLU_LIMIT)
        act = glu * jax.nn.sigmoid(SWIGLU_ALPHA * glu) * (lin + 1.0)
        out = out + gates[:, e:e + 1] * (act @ w_down[e] + b_down[e])
    return out.reshape(bsz, seq, d)


def setup_inputs(seed: int = 0) -> dict:
    key = jax.random.key(seed)
    ks = jax.random.split(key, 24)
    f32 = jnp.float32

    def nrm(k, shape, scale):
        return jax.random.normal(k, shape, f32) * scale

    L = DEPTH
    G, P, H = S5_GROUPS, S5_STATE, S5_GROUP
    n_idx = jnp.arange(P, dtype=f32)
    return {
        "x": nrm(ks[0], (BATCH, SEQ, D_MODEL), 1.0),
        "norm_mix": 1.0 + nrm(ks[1], (L, D_MODEL), 0.02),
        "w_in": nrm(ks[2], (L, D_MODEL, IN_WIDTH), D_MODEL ** -0.5),
        "s5_lam_re": -0.5 + nrm(ks[3], (L, G, P), 0.01),
        "s5_lam_im": jnp.pi * n_idx + nrm(ks[4], (L, G, P), 0.01),
        "s5_b_re": nrm(ks[5], (L, G, P, H), (2 * H) ** -0.5),
        "s5_b_im": nrm(ks[6], (L, G, P, H), (2 * H) ** -0.5),
        "s5_c_re": nrm(ks[7], (L, G, H, P), (2 * P) ** -0.5),
        "s5_c_im": nrm(ks[8], (L, G, H, P), (2 * P) ** -0.5),
        "s5_d": nrm(ks[9], (L, G, H), 1.0),
        "s5_log_step": jax.random.uniform(ks[10], (L, G), f32, float(np.log(S5_DT_MIN)), float(np.log(S5_DT_MAX))),
        "w_glu": nrm(ks[11], (L, S5_WIDTH, S5_WIDTH), S5_WIDTH ** -0.5),
        "w_branch_sb": nrm(ks[12], (L, SB_WIDTH, D_MODEL), SB_WIDTH ** -0.5),
        "w_branch_s5": nrm(ks[13], (L, S5_WIDTH, D_MODEL), S5_WIDTH ** -0.5),
        "w_out": nrm(ks[14], (L, D_MODEL, D_MODEL), D_MODEL ** -0.5),
        "norm_ffn": 1.0 + nrm(ks[15], (L, D_MODEL), 0.02),
        "w_router": nrm(ks[16], (L, D_MODEL, N_EXPERTS), D_MODEL ** -0.5),
        "b_router": nrm(ks[17], (L, N_EXPERTS), 0.01),
        "w_up": nrm(ks[18], (L, N_EXPERTS, D_MODEL, 2 * D_FF), D_MODEL ** -0.5),
        "b_up": nrm(ks[19], (L, N_EXPERTS, 2 * D_FF), 0.01),
        "w_down": nrm(ks[20], (L, N_EXPERTS, D_FF, D_MODEL), D_FF ** -0.5),
        "b_down": nrm(ks[21], (L, N_EXPERTS, D_MODEL), 0.01),
        "norm_final": 1.0 + nrm(ks[22], (D_MODEL,), 0.02),
    }


def reference(x, norm_mix, w_in, s5_lam_re, s5_lam_im, s5_b_re, s5_b_im, s5_c_re, s5_c_im,
              s5_d, s5_log_step, w_glu, w_branch_sb, w_branch_s5, w_out, norm_ffn,
              w_router, b_router, w_up, b_up, w_down, b_down, norm_final):
    bsz, seq, _ = x.shape
    for l in range(DEPTH):
        h = rmsnorm(x, norm_mix[l])
        proj = h @ w_in[l]
        q = split_heads(proj[..., :OFF_K])
        k = split_heads(proj[..., OFF_K:OFF_V])
        v = split_heads(proj[..., OFF_V:OFF_U])
        u = proj[..., OFF_U:OFF_GA]
        gate_sb = jax.nn.sigmoid(proj[..., OFF_GA:OFF_GB])
        gate_s5 = jax.nn.sigmoid(proj[..., OFF_GB:])
        y_sb = stick_breaking_attention(q, k, v).transpose(0, 2, 1, 3).reshape(bsz, seq, SB_WIDTH)
        y_s5 = s5_mixer(u, s5_lam_re[l], s5_lam_im[l], s5_b_re[l], s5_b_im[l], s5_c_re[l], s5_c_im[l],
                        s5_d[l], s5_log_step[l], w_glu[l])
        merged = gate_sb * (y_sb @ w_branch_sb[l]) + gate_s5 * (y_s5 @ w_branch_s5[l])
        x = x + (merged @ w_out[l]).astype(x.dtype)
        h = rmsnorm(x, norm_ffn[l])
        x = x + moe_ffn(h, w_router[l], b_router[l], w_up[l], b_up[l], w_down[l], b_down[l]).astype(x.dtype)
    return rmsnorm(x, norm_final).astype(x.dtype)
```

```python
import functools

import jax
import jax.numpy as jnp
from jax import lax
from jax.experimental import pallas as pl
from jax.experimental.pallas import tpu as pltpu

F32 = jnp.float32
BF16 = jnp.bfloat16

D_MODEL = 1024
SB_HEADS = 8
SB_HEAD_DIM = 64
SB_WIDTH = SB_HEADS * SB_HEAD_DIM
S5_WIDTH = 512
S5_GROUP = 16
S5_GROUPS = 32
S5_STATE = 64
N_EXPERTS = 32
TOP_K = 4
D_FF = 1024
SWIGLU_LIMIT = 7.0
SWIGLU_ALPHA = 1.702
RMS_EPS = 1e-5
OFF_U = 3 * SB_WIDTH
OFF_GATES = OFF_U + S5_WIDTH
IN_WIDTH = OFF_GATES + 2 * D_MODEL

LANES = 128
SUBLANES = 8
VMEM_LIMIT = 56 * 1024 * 1024

TM_PROJ = 512
TQ = 256
TK = 256
TC_S5 = 128
S5_BLOCKS = 4
TM_MERGE = 256
TM_MOE = 512


def _rmsnorm(x, g):
    return x * lax.rsqrt(jnp.mean(x * x, axis=-1, keepdims=True) + RMS_EPS) * g


def _sigmoid(x):
    return 1.0 / (1.0 + jnp.exp(-x))


def _inproj_kernel(x_ref, g_ref, w_ref, qkv_ref, u_ref, gate_ref):
    hb = _rmsnorm(x_ref[...], g_ref[...]).astype(BF16)
    chunk = SB_WIDTH

    def proj(c0):
        return jnp.dot(hb, w_ref[:, c0:c0 + chunk], preferred_element_type=F32)

    qkv_ref[:, 0:chunk] = (proj(0) * (SB_HEAD_DIM ** -0.5)).astype(BF16)
    qkv_ref[:, chunk:2 * chunk] = proj(chunk).astype(BF16)
    qkv_ref[:, 2 * chunk:3 * chunk] = proj(2 * chunk).astype(BF16)
    u_ref[...] = proj(OFF_U).astype(BF16)
    for c in range(2 * D_MODEL // chunk):
        gate_ref[:, c * chunk:(c + 1) * chunk] = proj(OFF_GATES + c * chunk).astype(BF16)


def _inproj(x2, g, w_bf):
    t = x2.shape[0]
    return pl.pallas_call(
        _inproj_kernel,
        grid=(t // TM_PROJ,),
        in_specs=[
            pl.BlockSpec((TM_PROJ, D_MODEL), lambda i: (i, 0)),
            pl.BlockSpec((1, D_MODEL), lambda i: (0, 0)),
            pl.BlockSpec((D_MODEL, IN_WIDTH), lambda i: (0, 0)),
        ],
        out_specs=[
            pl.BlockSpec((TM_PROJ, 3 * SB_WIDTH), lambda i: (i, 0)),
            pl.BlockSpec((TM_PROJ, S5_WIDTH), lambda i: (i, 0)),
            pl.BlockSpec((TM_PROJ, 2 * D_MODEL), lambda i: (i, 0)),
        ],
        out_shape=[
            jax.ShapeDtypeStruct((t, 3 * SB_WIDTH), BF16),
            jax.ShapeDtypeStruct((t, S5_WIDTH), BF16),
            jax.ShapeDtypeStruct((t, 2 * D_MODEL), BF16),
        ],
        compiler_params=pltpu.CompilerParams(
            dimension_semantics=("parallel",), vmem_limit_bytes=VMEM_LIMIT),
        name="inproj",
    )(x2, g, w_bf)


def _attn_kernel(q_ref, k_ref, v_ref, o_ref, acc_ref, cb_ref):
    qi = pl.program_id(2)
    q = q_ref[0]
    lane_head = lax.broadcasted_iota(jnp.int32, (TQ, LANES), 1) // SB_HEAD_DIM
    row = lax.broadcasted_iota(jnp.int32, (TQ, TK), 0)
    col = lax.broadcasted_iota(jnp.int32, (TQ, TK), 1)
    tri = jnp.where(row >= col, 1.0, 0.0).astype(BF16)
    causal = col < row
    acc_ref[...] = jnp.zeros_like(acc_ref)

    for h in range(LANES // SB_HEAD_DIM):
        head = lane_head == h
        qm = jnp.where(head, q, jnp.zeros_like(q))

        def tile(kt, diag):
            ks = pl.multiple_of(kt * TK, TK)
            k = k_ref[0, pl.ds(ks, TK), :]
            v = v_ref[0, pl.ds(ks, TK), :]
            z = lax.dot_general(qm, k, (((1,), (1,)), ((), ())), preferred_element_type=F32)
            log_keep = -(jnp.maximum(z, 0.0) + jnp.log(1.0 + jnp.exp(-jnp.abs(z))))
            if diag:
                log_keep = jnp.where(causal, log_keep, 0.0)
            hi = log_keep.astype(BF16)
            lo = (log_keep - hi.astype(F32)).astype(BF16)
            rcum = (jnp.dot(hi, tri, preferred_element_type=F32)
                    + jnp.dot(lo, tri, preferred_element_type=F32))
            w = jnp.exp(z + rcum + cb_ref[...])
            if diag:
                w = jnp.where(causal, w, 0.0)
            pv = jnp.dot(w.astype(BF16), v, preferred_element_type=F32)
            acc_ref[...] += jnp.where(head, pv, 0.0)
            cb_ref[...] += rcum[:, 0:1]

        cb_ref[...] = jnp.zeros_like(cb_ref)
        tile(qi, True)

        def body(j, carry):
            tile(qi - j, False)
            return carry

        lax.fori_loop(1, qi + 1, body, 0)

    o_ref[0] = acc_ref[...].astype(BF16)


def _attention(qkv3):
    b, l, _ = qkv3.shape
    pairs = SB_WIDTH // LANES
    return pl.pallas_call(
        _attn_kernel,
        grid=(b, pairs, l // TQ),
        in_specs=[
            pl.BlockSpec((1, TQ, LANES), lambda bi, hp, qi: (bi, qi, hp)),
            pl.BlockSpec((1, l, LANES), lambda bi, hp, qi: (bi, 0, pairs + hp)),
            pl.BlockSpec((1, l, LANES), lambda bi, hp, qi: (bi, 0, 2 * pairs + hp)),
        ],
        out_specs=pl.BlockSpec((1, TQ, LANES), lambda bi, hp, qi: (bi, qi, hp)),
        out_shape=jax.ShapeDtypeStruct((b, l, SB_WIDTH), BF16),
        scratch_shapes=[pltpu.VMEM((TQ, LANES), F32), pltpu.VMEM((TQ, 1), F32)],
        compiler_params=pltpu.CompilerParams(
            dimension_semantics=("parallel", "parallel", "arbitrary"),
            vmem_limit_bytes=VMEM_LIMIT),
        name="sb_attention",
    )(qkv3, qkv3, qkv3)


def _s5_kernel(u_ref, bblk_ref, cblk_ref, lam_a_ref, lam_b_ref, d_ref, wglu_ref, o_ref,
               s_ref, state_ref, y_ref):
    nb = u_ref.shape[0]
    half = S5_GROUPS * S5_STATE // S5_BLOCKS
    cin = S5_WIDTH // S5_BLOCKS
    tiles_per_block = half // LANES
    ntiles = S5_BLOCKS * tiles_per_block

    @pl.when(pl.program_id(0) == 0)
    def _():
        state_ref[...] = jnp.zeros_like(state_ref)

    for b in range(nb):
        re_rows = slice(b * TC_S5, (b + 1) * TC_S5)
        im_rows = slice((nb + b) * TC_S5, (nb + b + 1) * TC_S5)
        for blk in range(S5_BLOCKS):
            bu = jnp.dot(u_ref[b, :, blk * cin:(blk + 1) * cin], bblk_ref[blk],
                         preferred_element_type=F32)
            for i in range(tiles_per_block):
                j = blk * tiles_per_block + i
                s_ref[j, re_rows, :] = bu[:, i * LANES:(i + 1) * LANES]
                s_ref[j, im_rows, :] = bu[:, half + i * LANES:half + (i + 1) * LANES]

    def step(t, state):
        rows = pl.ds(t, 2 * nb, stride=TC_S5)
        new = []
        for j in range(ntiles):
            s = state[j]
            s = lam_a_ref[j] * s + lam_b_ref[j] * pltpu.roll(s, nb, axis=0) + s_ref[j, rows, :]
            s_ref[j, rows, :] = s
            new.append(s)
        return tuple(new)

    final = lax.fori_loop(0, TC_S5, step, tuple(state_ref[j] for j in range(ntiles)))
    for j in range(ntiles):
        state_ref[j] = final[j]

    for b in range(nb):
        re_rows = slice(b * TC_S5, (b + 1) * TC_S5)
        im_rows = slice((nb + b) * TC_S5, (nb + b + 1) * TC_S5)
        for blk in range(S5_BLOCKS):
            tiles = range(blk * tiles_per_block, (blk + 1) * tiles_per_block)
            sre = jnp.concatenate([s_ref[j, re_rows, :] for j in tiles], axis=1).astype(BF16)
            sim = jnp.concatenate([s_ref[j, im_rows, :] for j in tiles], axis=1).astype(BF16)
            y_ref[:, blk * cin:(blk + 1) * cin] = (
                jnp.dot(sre, cblk_ref[blk, :half, :], preferred_element_type=F32)
                + jnp.dot(sim, cblk_ref[blk, half:, :], preferred_element_type=F32))
        y = y_ref[...] + d_ref[...] * u_ref[b].astype(F32)
        y = jax.nn.gelu(y)
        glu = jnp.dot(y.astype(BF16), wglu_ref[...], preferred_element_type=F32)
        o_ref[b] = (y * _sigmoid(glu)).astype(BF16)


def _s5_params(lam_re, lam_im, b_re, b_im, c_re, c_im, log_step):
    g, p, h = S5_GROUPS, S5_STATE, S5_GROUP
    lam = lax.complex(lam_re.astype(F32), lam_im.astype(F32))
    step = jnp.exp(log_step.astype(F32))[:, None]
    lam_bar = jnp.exp(lam * step)
    b_bar = ((lam_bar - 1.0) / lam)[:, :, None] * lax.complex(b_re.astype(F32), b_im.astype(F32))
    gpb = g // S5_BLOCKS
    eye = jnp.eye(gpb, dtype=F32)

    def in_block(m):
        m = m.reshape(S5_BLOCKS, gpb, p, h)
        return jnp.einsum('kgph,gf->kghfp', m, eye).reshape(S5_BLOCKS, gpb * h, gpb * p)

    def out_block(m):
        m = m.reshape(S5_BLOCKS, gpb, h, p)
        return jnp.einsum('kghp,gf->kgpfh', m, eye).reshape(S5_BLOCKS, gpb * p, gpb * h)

    bblk = jnp.concatenate([in_block(jnp.real(b_bar)), in_block(jnp.imag(b_bar))], axis=2)
    cblk = jnp.concatenate([out_block(c_re.astype(F32)), out_block(-c_im.astype(F32))], axis=1)
    return bblk.astype(BF16), cblk.astype(BF16), jnp.real(lam_bar).reshape(1, g * p), jnp.imag(lam_bar).reshape(1, g * p)


def _s5(u3, bblk, cblk, lam_a, lam_b, d_row, wglu_bf):
    b, l, _ = u3.shape
    ntiles = S5_GROUPS * S5_STATE // LANES
    const = lambda *shape: pl.BlockSpec(shape, lambda i: (0,) * len(shape))
    return pl.pallas_call(
        _s5_kernel,
        grid=(l // TC_S5,),
        in_specs=[
            pl.BlockSpec((b, TC_S5, S5_WIDTH), lambda i: (0, i, 0)),
            const(*bblk.shape), const(*cblk.shape),
            const(ntiles, 2 * b, LANES), const(ntiles, 2 * b, LANES),
            const(1, S5_WIDTH), const(S5_WIDTH, S5_WIDTH),
        ],
        out_specs=pl.BlockSpec((b, TC_S5, S5_WIDTH), lambda i: (0, i, 0)),
        out_shape=jax.ShapeDtypeStruct((b, l, S5_WIDTH), BF16),
        scratch_shapes=[
            pltpu.VMEM((ntiles, 2 * b * TC_S5, LANES), F32),
            pltpu.VMEM((ntiles, 2 * b, LANES), F32),
            pltpu.VMEM((TC_S5, S5_WIDTH), F32),
        ],
        compiler_params=pltpu.CompilerParams(
            dimension_semantics=("arbitrary",), vmem_limit_bytes=VMEM_LIMIT),
        name="s5_scan",
    )(u3, bblk, cblk, lam_a, lam_b, d_row, wglu_bf)


def _merge_kernel(ysb_ref, ys5_ref, gate_ref, x_ref, wsb_ref, ws5_ref, wout_ref, g_ref,
                  wr_ref, br_ref, x1_ref, h2_ref, gt_ref):
    a = jnp.dot(ysb_ref[...], wsb_ref[...], preferred_element_type=F32)
    b = jnp.dot(ys5_ref[...], ws5_ref[...], preferred_element_type=F32)
    ga = _sigmoid(gate_ref[:, :D_MODEL].astype(F32))
    gb = _sigmoid(gate_ref[:, D_MODEL:].astype(F32))
    merged = (ga * a + gb * b).astype(BF16)
    x1 = x_ref[...] + jnp.dot(merged, wout_ref[...], preferred_element_type=F32)
    x1_ref[...] = x1
    h2 = _rmsnorm(x1, g_ref[...])
    h2_ref[...] = h2.astype(BF16)

    logits = lax.dot_general(wr_ref[...], h2, (((1,), (1,)), ((), ())),
                             precision=lax.Precision.HIGHEST,
                             preferred_element_type=F32) + br_ref[...]
    eidx = lax.broadcasted_iota(jnp.int32, logits.shape, 0)
    work = logits
    sel = jnp.zeros(logits.shape, jnp.bool_)
    top = None
    for k in range(TOP_K):
        m = jnp.max(work, axis=0, keepdims=True)
        first = jnp.min(jnp.where(work == m, eidx, N_EXPERTS), axis=0, keepdims=True)
        pick = eidx == first
        sel = jnp.logical_or(sel, pick)
        work = jnp.where(pick, -jnp.inf, work)
        if k == 0:
            top = m
    e = jnp.where(sel, jnp.exp(logits - top), 0.0)
    gt_ref[...] = e / jnp.sum(e, axis=0, keepdims=True)


def _merge(ysb, ys5, gates, x2, wsb, ws5, wout, g_ffn, wr_t, br_col):
    t = x2.shape[0]
    tm = TM_MERGE
    const = lambda *shape: pl.BlockSpec(shape, lambda i: (0,) * len(shape))
    return pl.pallas_call(
        _merge_kernel,
        grid=(t // tm,),
        in_specs=[
            pl.BlockSpec((tm, SB_WIDTH), lambda i: (i, 0)),
            pl.BlockSpec((tm, S5_WIDTH), lambda i: (i, 0)),
            pl.BlockSpec((tm, 2 * D_MODEL), lambda i: (i, 0)),
            pl.BlockSpec((tm, D_MODEL), lambda i: (i, 0)),
            const(SB_WIDTH, D_MODEL), const(S5_WIDTH, D_MODEL), const(D_MODEL, D_MODEL),
            const(1, D_MODEL), const(N_EXPERTS, D_MODEL), const(N_EXPERTS, 1),
        ],
        out_specs=[
            pl.BlockSpec((tm, D_MODEL), lambda i: (i, 0)),
            pl.BlockSpec((tm, D_MODEL), lambda i: (i, 0)),
            pl.BlockSpec((N_EXPERTS, tm), lambda i: (0, i)),
        ],
        out_shape=[
            jax.ShapeDtypeStruct((t, D_MODEL), F32),
            jax.ShapeDtypeStruct((t, D_MODEL), BF16),
            jax.ShapeDtypeStruct((N_EXPERTS, t), F32),
        ],
        compiler_params=pltpu.CompilerParams(
            dimension_semantics=("parallel",), vmem_limit_bytes=VMEM_LIMIT),
        name="merge_router",
    )(ysb, ys5, gates, x2, wsb, ws5, wout, g_ffn, wr_t, br_col)


def _moe_dense_kernel(h2_ref, x1_ref, gate_ref, wup_ref, bup_ref, wdn_ref, bdn_ref, gfin_ref,
                      o_ref, acc_ref):
    e = pl.program_id(1)

    @pl.when(e == 0)
    def _():
        acc_ref[...] = jnp.zeros_like(acc_ref)

    hid = jnp.dot(h2_ref[...], wup_ref[0].astype(BF16), preferred_element_type=F32) + bup_ref[0]
    glu = jnp.minimum(hid[:, :D_FF], SWIGLU_LIMIT)
    lin = jnp.clip(hid[:, D_FF:], -SWIGLU_LIMIT, SWIGLU_LIMIT)
    act = glu * _sigmoid(SWIGLU_ALPHA * glu) * (lin + 1.0)
    y = jnp.dot(act.astype(BF16), wdn_ref[0].astype(BF16), preferred_element_type=F32) + bdn_ref[0]
    acc_ref[...] += gate_ref[0] * y

    @pl.when(e == pl.num_programs(1) - 1)
    def _():
        o_ref[...] = _rmsnorm(x1_ref[...] + acc_ref[...], gfin_ref[...])


def _moe_dense(h2, x1, gate_col, w_up, b_up, w_down, b_down, g_fin):
    t = h2.shape[0]
    tm = TM_MOE
    return pl.pallas_call(
        _moe_dense_kernel,
        grid=(t // tm, N_EXPERTS),
        in_specs=[
            pl.BlockSpec((tm, D_MODEL), lambda i, e: (i, 0)),
            pl.BlockSpec((tm, D_MODEL), lambda i, e: (i, 0)),
            pl.BlockSpec((1, tm, 1), lambda i, e: (e, i, 0)),
            pl.BlockSpec((1, D_MODEL, 2 * D_FF), lambda i, e: (e, 0, 0)),
            pl.BlockSpec((1, 1, 2 * D_FF), lambda i, e: (e, 0, 0)),
            pl.BlockSpec((1, D_FF, D_MODEL), lambda i, e: (e, 0, 0)),
            pl.BlockSpec((1, 1, D_MODEL), lambda i, e: (e, 0, 0)),
            pl.BlockSpec((1, D_MODEL), lambda i, e: (0, 0)),
        ],
        out_specs=pl.BlockSpec((tm, D_MODEL), lambda i, e: (i, 0)),
        out_shape=jax.ShapeDtypeStruct((t, D_MODEL), F32),
        scratch_shapes=[pltpu.VMEM((tm, D_MODEL), F32)],
        compiler_params=pltpu.CompilerParams(
            dimension_semantics=("parallel", "arbitrary"), vmem_limit_bytes=VMEM_LIMIT),
        name="moe_dense",
    )(h2, x1, gate_col, w_up, b_up, w_down, b_down, g_fin)


def kernel(x, norm_mix, w_in, s5_lam_re, s5_lam_im, s5_b_re, s5_b_im, s5_c_re, s5_c_im, s5_d,
           s5_log_step, w_glu, w_branch_sb, w_branch_s5, w_out, norm_ffn, w_router, b_router,
           w_up, b_up, w_down, b_down, norm_final):
    bsz, seq, d = x.shape
    t = bsz * seq
    depth = norm_mix.shape[0]
    x2 = x.reshape(t, d).astype(F32)
    for l in range(depth):
        qkv, u, gates = _inproj(x2, norm_mix[l].reshape(1, d).astype(F32), w_in[l].astype(BF16))
        ysb = _attention(qkv.reshape(bsz, seq, 3 * SB_WIDTH))
        bblk, cblk, lbr, lbi = _s5_params(s5_lam_re[l], s5_lam_im[l], s5_b_re[l], s5_b_im[l],
                                          s5_c_re[l], s5_c_im[l], s5_log_step[l])
        ntiles = S5_GROUPS * S5_STATE // LANES
        lbr = lbr.reshape(ntiles, 1, LANES)
        lbi = lbi.reshape(ntiles, 1, LANES)
        lam_a = jnp.broadcast_to(lbr, (ntiles, 2 * bsz, LANES))
        lam_b = jnp.concatenate([jnp.broadcast_to(-lbi, (ntiles, bsz, LANES)),
                                 jnp.broadcast_to(lbi, (ntiles, bsz, LANES))], axis=1)
        ys5 = _s5(u.reshape(bsz, seq, S5_WIDTH), bblk, cblk, lam_a, lam_b,
                  s5_d[l].reshape(1, S5_WIDTH).astype(F32), w_glu[l].astype(BF16))
        x1, h2, gate_t = _merge(
            ysb.reshape(t, SB_WIDTH), ys5.reshape(t, S5_WIDTH), gates, x2,
            w_branch_sb[l].astype(BF16), w_branch_s5[l].astype(BF16), w_out[l].astype(BF16),
            norm_ffn[l].reshape(1, d).astype(F32), w_router[l].T.astype(F32),
            b_router[l].reshape(N_EXPERTS, 1).astype(F32))
        g_fin = norm_final if l == depth - 1 else jnp.ones((d,), F32)
        x2 = _moe_dense(h2, x1, gate_t.reshape(N_EXPERTS, t, 1), w_up[l],
                        b_up[l].reshape(N_EXPERTS, 1, 2 * D_FF), w_down[l],
                        b_down[l].reshape(N_EXPERTS, 1, D_MODEL),
                        g_fin.reshape(1, d).astype(F32))
    return x2.reshape(bsz, seq, d).astype(x.dtype)
```

```python
import functools

import jax
import jax.numpy as jnp
from jax import lax
from jax.experimental import pallas as pl
from jax.experimental.pallas import tpu as pltpu

F32 = jnp.float32
BF16 = jnp.bfloat16

D_MODEL = 1024
SB_HEADS = 8
SB_HEAD_DIM = 64
SB_WIDTH = SB_HEADS * SB_HEAD_DIM
S5_WIDTH = 512
S5_GROUP = 16
S5_GROUPS = 32
S5_STATE = 64
N_EXPERTS = 32
TOP_K = 4
D_FF = 1024
SWIGLU_LIMIT = 7.0
SWIGLU_ALPHA = 1.702
RMS_EPS = 1e-5
OFF_U = 3 * SB_WIDTH
OFF_GATES = OFF_U + S5_WIDTH
IN_WIDTH = OFF_GATES + 2 * D_MODEL

LANES = 128
SUBLANES = 8
VMEM_LIMIT = 56 * 1024 * 1024

TM_PROJ = 512
TQ = 256
TK = 256
EXP_UNDERFLOW = 110.0
TC_S5 = 128
S5_BLOCKS = 4
TM_MERGE = 256
TR = 256
TM_COMBINE = 256


def _rmsnorm(x, g):
    return x * lax.rsqrt(jnp.mean(x * x, axis=-1, keepdims=True) + RMS_EPS) * g


def _sigmoid(x):
    return 1.0 / (1.0 + jnp.exp(-x))


def _inproj_kernel(x_ref, g_ref, w_ref, qkv_ref, u_ref, gate_ref):
    hb = _rmsnorm(x_ref[...], g_ref[...]).astype(BF16)
    chunk = SB_WIDTH

    def proj(c0):
        return jnp.dot(hb, w_ref[:, c0:c0 + chunk], preferred_element_type=F32)

    qkv_ref[:, 0:chunk] = (proj(0) * (SB_HEAD_DIM ** -0.5)).astype(BF16)
    qkv_ref[:, chunk:2 * chunk] = proj(chunk).astype(BF16)
    qkv_ref[:, 2 * chunk:3 * chunk] = proj(2 * chunk).astype(BF16)
    u_ref[...] = proj(OFF_U).astype(BF16)
    for c in range(2 * D_MODEL // chunk):
        gate_ref[:, c * chunk:(c + 1) * chunk] = proj(OFF_GATES + c * chunk).astype(BF16)


def _inproj(x2, g, w_bf):
    t = x2.shape[0]
    return pl.pallas_call(
        _inproj_kernel,
        grid=(t // TM_PROJ,),
        in_specs=[
            pl.BlockSpec((TM_PROJ, D_MODEL), lambda i: (i, 0)),
            pl.BlockSpec((1, D_MODEL), lambda i: (0, 0)),
            pl.BlockSpec((D_MODEL, IN_WIDTH), lambda i: (0, 0)),
        ],
        out_specs=[
            pl.BlockSpec((TM_PROJ, 3 * SB_WIDTH), lambda i: (i, 0)),
            pl.BlockSpec((TM_PROJ, S5_WIDTH), lambda i: (i, 0)),
            pl.BlockSpec((TM_PROJ, 2 * D_MODEL), lambda i: (i, 0)),
        ],
        out_shape=[
            jax.ShapeDtypeStruct((t, 3 * SB_WIDTH), BF16),
            jax.ShapeDtypeStruct((t, S5_WIDTH), BF16),
            jax.ShapeDtypeStruct((t, 2 * D_MODEL), BF16),
        ],
        compiler_params=pltpu.CompilerParams(
            dimension_semantics=("parallel",), vmem_limit_bytes=VMEM_LIMIT),
        name="inproj",
    )(x2, g, w_bf)


def _attn_kernel(q_ref, k_ref, v_ref, o_ref, acc_ref, cb0_ref, cb1_ref):
    qi = pl.program_id(2)
    q = q_ref[0]
    is_h0 = lax.broadcasted_iota(jnp.int32, (TQ, LANES), 1) < SB_HEAD_DIM
    row = lax.broadcasted_iota(jnp.int32, (TQ, TK), 0)
    col = lax.broadcasted_iota(jnp.int32, (TQ, TK), 1)
    tri = jnp.where(row >= col, 1.0, 0.0).astype(BF16)
    causal = col < row
    zero = jnp.zeros_like(q)
    qms = (jnp.where(is_h0, q, zero), jnp.where(is_h0, zero, q))
    cbs = (cb0_ref, cb1_ref)
    acc_ref[...] = jnp.zeros_like(acc_ref)
    cb0_ref[...] = jnp.zeros_like(cb0_ref)
    cb1_ref[...] = jnp.zeros_like(cb1_ref)

    def tile(kt, diag):
        ks = pl.multiple_of(kt * TK, TK)
        k = k_ref[0, pl.ds(ks, TK), :]
        v = v_ref[0, pl.ds(ks, TK), :]
        pvs = []
        for qm, cb_ref in zip(qms, cbs):
            z = lax.dot_general(qm, k, (((1,), (1,)), ((), ())), preferred_element_type=F32)
            log_keep = -(jnp.maximum(z, 0.0) + jnp.log(1.0 + jnp.exp(-jnp.abs(z))))
            if diag:
                log_keep = jnp.where(causal, log_keep, 0.0)
            hi = log_keep.astype(BF16)
            lo = (log_keep - hi.astype(F32)).astype(BF16)
            rcum = (jnp.dot(hi, tri, preferred_element_type=F32)
                    + jnp.dot(lo, tri, preferred_element_type=F32))
            w = jnp.exp(z + rcum + cb_ref[...])
            if diag:
                w = jnp.where(causal, w, 0.0)
            pvs.append(jnp.dot(w.astype(BF16), v, preferred_element_type=F32))
            cb_ref[...] += rcum[:, 0:1]
        acc_ref[...] += jnp.where(is_h0, pvs[0], pvs[1])
        worst = jnp.maximum(jnp.max(cb0_ref[...]), jnp.max(cb1_ref[...]))
        return jnp.logical_not(worst < -EXP_UNDERFLOW)

    live = tile(qi, True)

    def cond(carry):
        j, live = carry
        return jnp.logical_and(j <= qi, live)

    def body(carry):
        j, _ = carry
        return j + 1, tile(qi - j, False)

    lax.while_loop(cond, body, (jnp.int32(1), live))
    o_ref[0] = acc_ref[...].astype(BF16)


def _attention(qkv3):
    b, l, _ = qkv3.shape
    pairs = SB_WIDTH // LANES
    return pl.pallas_call(
        _attn_kernel,
        grid=(b, pairs, l // TQ),
        in_specs=[
            pl.BlockSpec((1, TQ, LANES), lambda bi, hp, qi: (bi, qi, hp)),
            pl.BlockSpec((1, l, LANES), lambda bi, hp, qi: (bi, 0, pairs + hp)),
            pl.BlockSpec((1, l, LANES), lambda bi, hp, qi: (bi, 0, 2 * pairs + hp)),
        ],
        out_specs=pl.BlockSpec((1, TQ, LANES), lambda bi, hp, qi: (bi, qi, hp)),
        out_shape=jax.ShapeDtypeStruct((b, l, SB_WIDTH), BF16),
        scratch_shapes=[pltpu.VMEM((TQ, LANES), F32), pltpu.VMEM((TQ, 1), F32),
                        pltpu.VMEM((TQ, 1), F32)],
        compiler_params=pltpu.CompilerParams(
            dimension_semantics=("parallel", "parallel", "arbitrary"),
            vmem_limit_bytes=VMEM_LIMIT),
        name="sb_attention",
    )(qkv3, qkv3, qkv3)


def _s5_kernel(u_ref, bblk_ref, cblk_ref, lam_a_ref, lam_b_ref, d_ref, wglu_ref, o_ref,
               s_ref, state_ref, y_ref):
    nb = u_ref.shape[0]
    half = S5_GROUPS * S5_STATE // S5_BLOCKS
    cin = S5_WIDTH // S5_BLOCKS
    tiles_per_block = half // LANES
    ntiles = S5_BLOCKS * tiles_per_block

    @pl.when(pl.program_id(0) == 0)
    def _():
        state_ref[...] = jnp.zeros_like(state_ref)

    for b in range(nb):
        re_rows = slice(b * TC_S5, (b + 1) * TC_S5)
        im_rows = slice((nb + b) * TC_S5, (nb + b + 1) * TC_S5)
        for blk in range(S5_BLOCKS):
            bu = jnp.dot(u_ref[b, :, blk * cin:(blk + 1) * cin], bblk_ref[blk],
                         preferred_element_type=F32)
            for i in range(tiles_per_block):
                j = blk * tiles_per_block + i
                s_ref[j, re_rows, :] = bu[:, i * LANES:(i + 1) * LANES]
                s_ref[j, im_rows, :] = bu[:, half + i * LANES:half + (i + 1) * LANES]

    def step(t, state):
        rows = pl.ds(t, 2 * nb, stride=TC_S5)
        new = []
        for j in range(ntiles):
            s = state[j]
            s = lam_a_ref[j] * s + lam_b_ref[j] * pltpu.roll(s, nb, axis=0) + s_ref[j, rows, :]
            s_ref[j, rows, :] = s
            new.append(s)
        return tuple(new)

    final = lax.fori_loop(0, TC_S5, step, tuple(state_ref[j] for j in range(ntiles)))
    for j in range(ntiles):
        state_ref[j] = final[j]

    for b in range(nb):
        re_rows = slice(b * TC_S5, (b + 1) * TC_S5)
        im_rows = slice((nb + b) * TC_S5, (nb + b + 1) * TC_S5)
        for blk in range(S5_BLOCKS):
            tiles = range(blk * tiles_per_block, (blk + 1) * tiles_per_block)
            sre = jnp.concatenate([s_ref[j, re_rows, :] for j in tiles], axis=1).astype(BF16)
            sim = jnp.concatenate([s_ref[j, im_rows, :] for j in tiles], axis=1).astype(BF16)
            y_ref[:, blk * cin:(blk + 1) * cin] = (
                jnp.dot(sre, cblk_ref[blk, :half, :], preferred_element_type=F32)
                + jnp.dot(sim, cblk_ref[blk, half:, :], preferred_element_type=F32))
        y = y_ref[...] + d_ref[...] * u_ref[b].astype(F32)
        y = jax.nn.gelu(y)
        glu = jnp.dot(y.astype(BF16), wglu_ref[...], preferred_element_type=F32)
        o_ref[b] = (y * _sigmoid(glu)).astype(BF16)


def _s5_params(lam_re, lam_im, b_re, b_im, c_re, c_im, log_step):
    g, p, h = S5_GROUPS, S5_STATE, S5_GROUP
    lr, li = lam_re.astype(F32), lam_im.astype(F32)
    step = jnp.exp(log_step.astype(F32))[:, None]
    mag = jnp.exp(lr * step)
    bar_re, bar_im = mag * jnp.cos(li * step), mag * jnp.sin(li * step)
    den = lr * lr + li * li
    f_re = ((bar_re - 1.0) * lr + bar_im * li) / den
    f_im = (bar_im * lr - (bar_re - 1.0) * li) / den
    f_re, f_im = f_re[:, :, None], f_im[:, :, None]
    bb_re = f_re * b_re.astype(F32) - f_im * b_im.astype(F32)
    bb_im = f_re * b_im.astype(F32) + f_im * b_re.astype(F32)
    gpb = g // S5_BLOCKS
    eye = jnp.eye(gpb, dtype=F32)

    def in_block(m):
        m = m.reshape(S5_BLOCKS, gpb, p, h)
        return jnp.einsum('kgph,gf->kghfp', m, eye).reshape(S5_BLOCKS, gpb * h, gpb * p)

    def out_block(m):
        m = m.reshape(S5_BLOCKS, gpb, h, p)
        return jnp.einsum('kghp,gf->kgpfh', m, eye).reshape(S5_BLOCKS, gpb * p, gpb * h)

    bblk = jnp.concatenate([in_block(bb_re), in_block(bb_im)], axis=2)
    cblk = jnp.concatenate([out_block(c_re.astype(F32)), out_block(-c_im.astype(F32))], axis=1)
    return bblk.astype(BF16), cblk.astype(BF16), bar_re.reshape(1, g * p), bar_im.reshape(1, g * p)


def _s5(u3, bblk, cblk, lam_a, lam_b, d_row, wglu_bf):
    b, l, _ = u3.shape
    ntiles = S5_GROUPS * S5_STATE // LANES
    const = lambda *shape: pl.BlockSpec(shape, lambda i: (0,) * len(shape))
    return pl.pallas_call(
        _s5_kernel,
        grid=(l // TC_S5,),
        in_specs=[
            pl.BlockSpec((b, TC_S5, S5_WIDTH), lambda i: (0, i, 0)),
            const(*bblk.shape), const(*cblk.shape),
            const(ntiles, 2 * b, LANES), const(ntiles, 2 * b, LANES),
            const(1, S5_WIDTH), const(S5_WIDTH, S5_WIDTH),
        ],
        out_specs=pl.BlockSpec((b, TC_S5, S5_WIDTH), lambda i: (0, i, 0)),
        out_shape=jax.ShapeDtypeStruct((b, l, S5_WIDTH), BF16),
        scratch_shapes=[
            pltpu.VMEM((ntiles, 2 * b * TC_S5, LANES), F32),
            pltpu.VMEM((ntiles, 2 * b, LANES), F32),
            pltpu.VMEM((TC_S5, S5_WIDTH), F32),
        ],
        compiler_params=pltpu.CompilerParams(
            dimension_semantics=("arbitrary",), vmem_limit_bytes=VMEM_LIMIT),
        name="s5_scan",
    )(u3, bblk, cblk, lam_a, lam_b, d_row, wglu_bf)


def _merge_kernel(ysb_ref, ys5_ref, gate_ref, x_ref, wsb_ref, ws5_ref, wout_ref, g_ref,
                  wr_ref, br_ref, x1_ref, h2_ref, ek_ref, pk_ref, gk_ref, cnt_ref, carry_ref):
    @pl.when(pl.program_id(0) == 0)
    def _():
        carry_ref[...] = jnp.zeros_like(carry_ref)

    a = jnp.dot(ysb_ref[...], wsb_ref[...], preferred_element_type=F32)
    b = jnp.dot(ys5_ref[...], ws5_ref[...], preferred_element_type=F32)
    ga = _sigmoid(gate_ref[:, :D_MODEL].astype(F32))
    gb = _sigmoid(gate_ref[:, D_MODEL:].astype(F32))
    merged = (ga * a + gb * b).astype(BF16)
    x1 = x_ref[...] + jnp.dot(merged, wout_ref[...], preferred_element_type=F32)
    x1_ref[...] = x1
    h2 = _rmsnorm(x1, g_ref[...])
    h2_ref[...] = h2

    logits = lax.dot_general(wr_ref[...], h2, (((1,), (1,)), ((), ())),
                             precision=lax.Precision.HIGHEST,
                             preferred_element_type=F32) + br_ref[...]
    eidx = lax.broadcasted_iota(jnp.int32, logits.shape, 0)
    work = logits
    sel = jnp.zeros(logits.shape, jnp.bool_)
    top = None
    for k in range(TOP_K):
        m = jnp.max(work, axis=0, keepdims=True)
        first = jnp.min(jnp.where(work == m, eidx, N_EXPERTS), axis=0, keepdims=True)
        pick = eidx == first
        sel = jnp.logical_or(sel, pick)
        work = jnp.where(pick, -jnp.inf, work)
        if k == 0:
            top = m
    e = jnp.where(sel, jnp.exp(logits - top), 0.0)
    gate = e / jnp.sum(e, axis=0, keepdims=True)

    tm = logits.shape[1]
    mask = jnp.where(sel, 1.0, 0.0)
    trow = lax.broadcasted_iota(jnp.int32, (tm, tm), 0)
    tcol = lax.broadcasted_iota(jnp.int32, (tm, tm), 1)
    before = jnp.where(trow < tcol, 1.0, 0.0).astype(BF16)
    pos = jnp.dot(mask.astype(BF16), before, preferred_element_type=F32) + carry_ref[...]
    carry_ref[...] += jnp.sum(mask, axis=1, keepdims=True)
    cnt_ref[...] = jnp.broadcast_to(carry_ref[...], cnt_ref.shape)

    erow = lax.broadcasted_iota(jnp.int32, (N_EXPERTS, N_EXPERTS), 0)
    ecol = lax.broadcasted_iota(jnp.int32, (N_EXPERTS, N_EXPERTS), 1)
    lower = jnp.where(ecol < erow, 1.0, 0.0).astype(BF16)
    rank = jnp.dot(lower, mask.astype(BF16), preferred_element_type=F32)
    eidx_f = eidx.astype(F32)
    for k in range(TOP_K):
        ind = jnp.logical_and(sel, rank == float(k))
        pick1 = lambda v: jnp.sum(jnp.where(ind, v, 0.0), axis=0, keepdims=True)
        ek_ref[k:k + 1, :] = pick1(eidx_f).astype(jnp.int32)
        pk_ref[k:k + 1, :] = pick1(pos).astype(jnp.int32)
        gk_ref[k:k + 1, :] = pick1(gate)


def _merge(ysb, ys5, gates, x2, wsb, ws5, wout, g_ffn, wr_t, br_col):
    t = x2.shape[0]
    tm = TM_MERGE
    const = lambda *shape: pl.BlockSpec(shape, lambda i: (0,) * len(shape))
    return pl.pallas_call(
        _merge_kernel,
        grid=(t // tm,),
        in_specs=[
            pl.BlockSpec((tm, SB_WIDTH), lambda i: (i, 0)),
            pl.BlockSpec((tm, S5_WIDTH), lambda i: (i, 0)),
            pl.BlockSpec((tm, 2 * D_MODEL), lambda i: (i, 0)),
            pl.BlockSpec((tm, D_MODEL), lambda i: (i, 0)),
            const(SB_WIDTH, D_MODEL), const(S5_WIDTH, D_MODEL), const(D_MODEL, D_MODEL),
            const(1, D_MODEL), const(N_EXPERTS, D_MODEL), const(N_EXPERTS, 1),
        ],
        out_specs=[
            pl.BlockSpec((tm, D_MODEL), lambda i: (i, 0)),
            pl.BlockSpec((tm, D_MODEL), lambda i: (i, 0)),
            pl.BlockSpec((TOP_K, tm), lambda i: (0, i)),
            pl.BlockSpec((TOP_K, tm), lambda i: (0, i)),
            pl.BlockSpec((TOP_K, tm), lambda i: (0, i)),
            pl.BlockSpec((N_EXPERTS, LANES), lambda i: (0, 0)),
        ],
        out_shape=[
            jax.ShapeDtypeStruct((t, D_MODEL), F32),
            jax.ShapeDtypeStruct((t, D_MODEL), F32),
            jax.ShapeDtypeStruct((TOP_K, t), jnp.int32),
            jax.ShapeDtypeStruct((TOP_K, t), jnp.int32),
            jax.ShapeDtypeStruct((TOP_K, t), F32),
            jax.ShapeDtypeStruct((N_EXPERTS, LANES), F32),
        ],
        scratch_shapes=[pltpu.VMEM((N_EXPERTS, 1), F32)],
        compiler_params=pltpu.CompilerParams(
            dimension_semantics=("arbitrary",), vmem_limit_bytes=VMEM_LIMIT),
        name="merge_router",
    )(ysb, ys5, gates, x2, wsb, ws5, wout, g_ffn, wr_t, br_col)


def _moe_kernel(slot_ref, te_ref, nreal_ref, nused_ref, h2_hbm, wup_ref, bup_ref, wdn_ref, bdn_ref,
                yt_hbm, xbuf, ybuf, wup_bf, wdn_bf, rows_ref, gsem, ssem):
    r = pl.program_id(0)
    nused = nused_ref[0]
    t_total = h2_hbm.shape[0]

    def gather_start(tile, buf):
        def body(i, carry):
            tok = lax.rem(slot_ref[tile * TR + i], t_total)
            pltpu.make_async_copy(h2_hbm.at[pl.ds(tok, 1), :], xbuf.at[buf, pl.ds(i, 1), :],
                                  gsem.at[buf]).start()
            return carry
        lax.fori_loop(0, TR, body, 0)

    def gather_wait(buf):
        pltpu.make_async_copy(h2_hbm.at[pl.ds(0, TR), :], xbuf.at[buf], gsem.at[buf]).wait()

    def scatter_start(tile, buf):
        def body(i, carry):
            dst = slot_ref[tile * TR + i]
            pltpu.make_async_copy(ybuf.at[buf, pl.ds(i, 1), :], yt_hbm.at[pl.ds(dst, 1), :],
                                  ssem.at[buf]).start()
            return carry
        lax.fori_loop(0, nreal_ref[tile], body, 0)

    def scatter_wait(tile, buf):
        n = nreal_ref[tile]

        @pl.when(n > 0)
        def _():
            pltpu.make_async_copy(rows_ref.at[pl.ds(0, n)], rows_ref.at[pl.ds(0, n)],
                                  ssem.at[buf]).wait()

    @pl.when(r == 0)
    def _():
        gather_start(0, 0)

    @pl.when(r + 1 < nused)
    def _():
        gather_start(r + 1, lax.rem(r + 1, 2))

    @pl.when(r < nused)
    def _():
        buf = lax.rem(r, 2)
        gather_wait(buf)

        @pl.when(jnp.logical_or(r == 0, te_ref[r] != te_ref[jnp.maximum(r - 1, 0)]))
        def _():
            wup_bf[...] = wup_ref[0].astype(BF16)
            wdn_bf[...] = wdn_ref[0].astype(BF16)

        x = xbuf[buf].astype(BF16)
        hid = jnp.dot(x, wup_bf[...], preferred_element_type=F32) + bup_ref[0]
        glu = jnp.minimum(hid[:, :D_FF], SWIGLU_LIMIT)
        lin = jnp.clip(hid[:, D_FF:], -SWIGLU_LIMIT, SWIGLU_LIMIT)
        act = glu * _sigmoid(SWIGLU_ALPHA * glu) * (lin + 1.0)
        y = jnp.dot(act.astype(BF16), wdn_bf[...], preferred_element_type=F32) + bdn_ref[0]

        @pl.when(r >= 2)
        def _():
            scatter_wait(r - 2, buf)

        ybuf[buf] = y
        scatter_start(r, buf)

    @pl.when(jnp.logical_and(r == nused - 1, r >= 1))
    def _():
        scatter_wait(r - 1, lax.rem(r + 1, 2))

    @pl.when(r == nused - 1)
    def _():
        scatter_wait(r, lax.rem(r, 2))


def _moe(slot_of_row, tile_expert, tile_nreal, nused, h2, w_up, b_up, w_down, b_down):
    t = h2.shape[0]
    ntile = tile_expert.shape[0]
    by_expert = lambda r, slot, te, nreal, nu: (te[r], 0, 0)
    grid_spec = pltpu.PrefetchScalarGridSpec(
        num_scalar_prefetch=4,
        grid=(ntile,),
        in_specs=[
            pl.BlockSpec(memory_space=pl.ANY),
            pl.BlockSpec((1, D_MODEL, 2 * D_FF), by_expert),
            pl.BlockSpec((1, 1, 2 * D_FF), by_expert),
            pl.BlockSpec((1, D_FF, D_MODEL), by_expert),
            pl.BlockSpec((1, 1, D_MODEL), by_expert),
        ],
        out_specs=pl.BlockSpec(memory_space=pl.ANY),
        scratch_shapes=[
            pltpu.VMEM((2, TR, D_MODEL), F32),
            pltpu.VMEM((2, TR, D_MODEL), F32),
            pltpu.VMEM((D_MODEL, 2 * D_FF), BF16),
            pltpu.VMEM((D_FF, D_MODEL), BF16),
            pltpu.VMEM((TR, D_MODEL // LANES, LANES), F32),
            pltpu.SemaphoreType.DMA((2,)),
            pltpu.SemaphoreType.DMA((2,)),
        ],
    )
    return pl.pallas_call(
        _moe_kernel,
        grid_spec=grid_spec,
        out_shape=jax.ShapeDtypeStruct((TOP_K * t, D_MODEL), F32),
        compiler_params=pltpu.CompilerParams(
            dimension_semantics=("arbitrary",), vmem_limit_bytes=VMEM_LIMIT),
        name="moe_experts",
    )(slot_of_row, tile_expert, tile_nreal, nused, h2, w_up, b_up, w_down, b_down)


def _route(ek, pk, cnt, t):
    ntile = TOP_K * t // TR + N_EXPERTS
    counts = cnt[:, 0].astype(jnp.int32)
    padded = (counts + TR - 1) // TR * TR
    ends = jnp.cumsum(padded)
    starts = ends - padded
    dest = jnp.take(starts, ek) + pk
    slot = (jnp.arange(TOP_K, dtype=jnp.int32)[:, None] * t
            + jnp.arange(t, dtype=jnp.int32)[None, :])
    nrow = ntile * TR
    slot_of_row = jnp.zeros((nrow,), jnp.int32).at[dest.reshape(-1)].set(
        slot.reshape(-1), unique_indices=True)
    nused = ends[-1] // TR
    tile_start = jnp.arange(ntile, dtype=jnp.int32) * TR
    tile_expert = jnp.sum(tile_start[:, None] >= ends[None, :], axis=1).astype(jnp.int32)
    tile_expert = jnp.minimum(tile_expert, N_EXPERTS - 1)
    tile_nreal = jnp.clip(jnp.take(starts + counts, tile_expert) - tile_start, 0, TR)
    tile_nreal = jnp.where(tile_start < ends[-1], tile_nreal, 0).astype(jnp.int32)
    last = jnp.take(tile_expert, nused - 1)
    tile_expert = jnp.minimum(tile_expert, last)
    return slot_of_row, tile_expert, tile_nreal, nused.reshape(1).astype(jnp.int32)


def _combine_kernel(x1_ref, y0_ref, y1_ref, y2_ref, y3_ref, gk_ref, gfin_ref, o_ref):
    acc = x1_ref[...]
    for k, y_ref in enumerate((y0_ref, y1_ref, y2_ref, y3_ref)):
        acc = acc + gk_ref[k] * y_ref[...]
    o_ref[...] = _rmsnorm(acc, gfin_ref[...])


def _combine(x1, yt, gk_col, g_fin):
    t = x1.shape[0]
    tm = TM_COMBINE
    nblk = t // tm
    y_spec = lambda k: pl.BlockSpec((tm, D_MODEL), lambda i: (k * nblk + i, 0))
    return pl.pallas_call(
        _combine_kernel,
        grid=(nblk,),
        in_specs=[
            pl.BlockSpec((tm, D_MODEL), lambda i: (i, 0)),
            y_spec(0), y_spec(1), y_spec(2), y_spec(3),
            pl.BlockSpec((TOP_K, tm, 1), lambda i: (0, i, 0)),
            pl.BlockSpec((1, D_MODEL), lambda i: (0, 0)),
        ],
        out_specs=pl.BlockSpec((tm, D_MODEL), lambda i: (i, 0)),
        out_shape=jax.ShapeDtypeStruct((t, D_MODEL), F32),
        compiler_params=pltpu.CompilerParams(
            dimension_semantics=("parallel",), vmem_limit_bytes=VMEM_LIMIT),
        name="combine_norm",
    )(x1, yt, yt, yt, yt, gk_col, g_fin)


def kernel(x, norm_mix, w_in, s5_lam_re, s5_lam_im, s5_b_re, s5_b_im, s5_c_re, s5_c_im, s5_d,
           s5_log_step, w_glu, w_branch_sb, w_branch_s5, w_out, norm_ffn, w_router, b_router,
           w_up, b_up, w_down, b_down, norm_final):
    bsz, seq, d = x.shape
    t = bsz * seq
    assert norm_mix.shape[0] == 1, "single-layer block"
    x2 = x.reshape(t, d).astype(F32)
    for l in range(1):
        qkv, u, gates = _inproj(x2, norm_mix[l].reshape(1, d).astype(F32), w_in[l].astype(BF16))
        ysb = _attention(qkv.reshape(bsz, seq, 3 * SB_WIDTH))
        bblk, cblk, lbr, lbi = _s5_params(s5_lam_re[l], s5_lam_im[l], s5_b_re[l], s5_b_im[l],
                                          s5_c_re[l], s5_c_im[l], s5_log_step[l])
        ntiles = S5_GROUPS * S5_STATE // LANES
        lbr = lbr.reshape(ntiles, 1, LANES)
        lbi = lbi.reshape(ntiles, 1, LANES)
        lam_a = jnp.broadcast_to(lbr, (ntiles, 2 * bsz, LANES))
        lam_b = jnp.concatenate([jnp.broadcast_to(-lbi, (ntiles, bsz, LANES)),
                                 jnp.broadcast_to(lbi, (ntiles, bsz, LANES))], axis=1)
        ys5 = _s5(u.reshape(bsz, seq, S5_WIDTH), bblk, cblk, lam_a, lam_b,
                  s5_d[l].reshape(1, S5_WIDTH).astype(F32), w_glu[l].astype(BF16))
        x1, h2, ek, pk, gk, cnt = _merge(
            ysb.reshape(t, SB_WIDTH), ys5.reshape(t, S5_WIDTH), gates, x2,
            w_branch_sb[l].astype(BF16), w_branch_s5[l].astype(BF16), w_out[l].astype(BF16),
            norm_ffn[l].reshape(1, d).astype(F32), w_router[l].T.astype(F32),
            b_router[l].reshape(N_EXPERTS, 1).astype(F32))
        slot_of_row, tile_expert, tile_nreal, nused = _route(ek, pk, cnt, t)
        yt = _moe(slot_of_row, tile_expert, tile_nreal, nused, h2, w_up[l],
                  b_up[l].reshape(N_EXPERTS, 1, 2 * D_FF), w_down[l],
                  b_down[l].reshape(N_EXPERTS, 1, D_MODEL))
        x2 = _combine(x1, yt, gk.reshape(TOP_K, t, 1), norm_final.reshape(1, d).astype(F32))
    return x2.reshape(bsz, seq, d).astype(x.dtype)
```

```python
import functools

import jax
import jax.numpy as jnp
from jax import lax
from jax.experimental import pallas as pl
from jax.experimental.pallas import tpu as pltpu

F32 = jnp.float32
BF16 = jnp.bfloat16

D_MODEL = 1024
SB_HEADS = 8
SB_HEAD_DIM = 64
SB_WIDTH = SB_HEADS * SB_HEAD_DIM
S5_WIDTH = 512
S5_GROUP = 16
S5_GROUPS = 32
S5_STATE = 64
N_EXPERTS = 32
TOP_K = 4
D_FF = 1024
SWIGLU_LIMIT = 7.0
SWIGLU_ALPHA = 1.702
RMS_EPS = 1e-5
OFF_U = 3 * SB_WIDTH
OFF_GATES = OFF_U + S5_WIDTH
IN_WIDTH = OFF_GATES + 2 * D_MODEL

LANES = 128
SUBLANES = 8
VMEM_LIMIT = 56 * 1024 * 1024

TM_PROJ = 512
TQ = 256
TK = 256
EXP_UNDERFLOW = 110.0
TC_S5 = 256
S5_BLOCKS = 4
TM_MERGE = 256
TR = 256
TM_COMBINE = 256
TM_DISPATCH = 256


def _rmsnorm(x, g):
    return x * lax.rsqrt(jnp.mean(x * x, axis=-1, keepdims=True) + RMS_EPS) * g


def _sigmoid(x):
    return 1.0 / (1.0 + jnp.exp(-x))


def _inproj_kernel(x_ref, g_ref, w_ref, qkv_ref, u_ref, gate_ref):
    hb = _rmsnorm(x_ref[...], g_ref[...]).astype(BF16)
    chunk = SB_WIDTH

    def proj(c0):
        return jnp.dot(hb, w_ref[:, c0:c0 + chunk], preferred_element_type=F32)

    qkv_ref[:, 0:chunk] = (proj(0) * (SB_HEAD_DIM ** -0.5)).astype(BF16)
    qkv_ref[:, chunk:2 * chunk] = proj(chunk).astype(BF16)
    qkv_ref[:, 2 * chunk:3 * chunk] = proj(2 * chunk).astype(BF16)
    u_ref[...] = proj(OFF_U).astype(BF16)
    for c in range(2 * D_MODEL // chunk):
        gate_ref[:, c * chunk:(c + 1) * chunk] = proj(OFF_GATES + c * chunk).astype(BF16)


def _inproj(x2, g, w_bf):
    t = x2.shape[0]
    return pl.pallas_call(
        _inproj_kernel,
        grid=(t // TM_PROJ,),
        in_specs=[
            pl.BlockSpec((TM_PROJ, D_MODEL), lambda i: (i, 0)),
            pl.BlockSpec((1, D_MODEL), lambda i: (0, 0)),
            pl.BlockSpec((D_MODEL, IN_WIDTH), lambda i: (0, 0)),
        ],
        out_specs=[
            pl.BlockSpec((TM_PROJ, 3 * SB_WIDTH), lambda i: (i, 0)),
            pl.BlockSpec((TM_PROJ, S5_WIDTH), lambda i: (i, 0)),
            pl.BlockSpec((TM_PROJ, 2 * D_MODEL), lambda i: (i, 0)),
        ],
        out_shape=[
            jax.ShapeDtypeStruct((t, 3 * SB_WIDTH), BF16),
            jax.ShapeDtypeStruct((t, S5_WIDTH), BF16),
            jax.ShapeDtypeStruct((t, 2 * D_MODEL), BF16),
        ],
        compiler_params=pltpu.CompilerParams(
            dimension_semantics=("parallel",), vmem_limit_bytes=VMEM_LIMIT),
        name="inproj",
    )(x2, g, w_bf)


def _attn_kernel(q_ref, k_ref, v_ref, o_ref, acc_ref, cb0_ref, cb1_ref):
    qi = pl.program_id(2)
    q = q_ref[0]
    is_h0 = lax.broadcasted_iota(jnp.int32, (TQ, LANES), 1) < SB_HEAD_DIM
    row = lax.broadcasted_iota(jnp.int32, (TQ, TK), 0)
    col = lax.broadcasted_iota(jnp.int32, (TQ, TK), 1)
    tri = jnp.where(row >= col, 1.0, 0.0).astype(BF16)
    causal = col < row
    zero = jnp.zeros_like(q)
    qms = (jnp.where(is_h0, q, zero), jnp.where(is_h0, zero, q))
    cbs = (cb0_ref, cb1_ref)
    acc_ref[...] = jnp.zeros_like(acc_ref)
    cb0_ref[...] = jnp.zeros_like(cb0_ref)
    cb1_ref[...] = jnp.zeros_like(cb1_ref)

    def tile(kt, diag):
        ks = pl.multiple_of(kt * TK, TK)
        k = k_ref[0, pl.ds(ks, TK), :]
        v = v_ref[0, pl.ds(ks, TK), :]
        pvs = []
        for qm, cb_ref in zip(qms, cbs):
            z = lax.dot_general(qm, k, (((1,), (1,)), ((), ())), preferred_element_type=F32)
            log_keep = -(jnp.maximum(z, 0.0) + jnp.log(1.0 + jnp.exp(-jnp.abs(z))))
            if diag:
                log_keep = jnp.where(causal, log_keep, 0.0)
            hi = log_keep.astype(BF16)
            lo = (log_keep - hi.astype(F32)).astype(BF16)
            rcum = (jnp.dot(hi, tri, preferred_element_type=F32)
                    + jnp.dot(lo, tri, preferred_element_type=F32))
            w = jnp.exp(z + rcum + cb_ref[...])
            if diag:
                w = jnp.where(causal, w, 0.0)
            pvs.append(jnp.dot(w.astype(BF16), v, preferred_element_type=F32))
            cb_ref[...] += rcum[:, 0:1]
        acc_ref[...] += jnp.where(is_h0, pvs[0], pvs[1])
        worst = jnp.maximum(jnp.max(cb0_ref[...]), jnp.max(cb1_ref[...]))
        return jnp.logical_not(worst < -EXP_UNDERFLOW)

    live = tile(qi, True)

    def cond(carry):
        j, live = carry
        return jnp.logical_and(j <= qi, live)

    def body(carry):
        j, _ = carry
        return j + 1, tile(qi - j, False)

    lax.while_loop(cond, body, (jnp.int32(1), live))
    o_ref[0] = acc_ref[...].astype(BF16)


def _attention(qkv3):
    b, l, _ = qkv3.shape
    pairs = SB_WIDTH // LANES
    return pl.pallas_call(
        _attn_kernel,
        grid=(b, pairs, l // TQ),
        in_specs=[
            pl.BlockSpec((1, TQ, LANES), lambda bi, hp, qi: (bi, qi, hp)),
            pl.BlockSpec((1, l, LANES), lambda bi, hp, qi: (bi, 0, pairs + hp)),
            pl.BlockSpec((1, l, LANES), lambda bi, hp, qi: (bi, 0, 2 * pairs + hp)),
        ],
        out_specs=pl.BlockSpec((1, TQ, LANES), lambda bi, hp, qi: (bi, qi, hp)),
        out_shape=jax.ShapeDtypeStruct((b, l, SB_WIDTH), BF16),
        scratch_shapes=[pltpu.VMEM((TQ, LANES), F32), pltpu.VMEM((TQ, 1), F32),
                        pltpu.VMEM((TQ, 1), F32)],
        compiler_params=pltpu.CompilerParams(
            dimension_semantics=("parallel", "parallel", "arbitrary"),
            vmem_limit_bytes=VMEM_LIMIT),
        name="sb_attention",
    )(qkv3, qkv3, qkv3)


def _s5_kernel(u_ref, bblk_ref, cblk_ref, tab_ref, d_ref, wglu_ref, o_ref,
               sre_ref, sim_ref, state_ref, y_ref):
    nb = u_ref.shape[0]
    half = S5_GROUPS * S5_STATE // S5_BLOCKS
    cin = S5_WIDTH // S5_BLOCKS
    ntiles = S5_GROUPS * S5_STATE // LANES

    @pl.when(pl.program_id(0) == 0)
    def _():
        state_ref[...] = jnp.zeros_like(state_ref)

    for b in range(nb):
        for blk in range(S5_BLOCKS):
            bu = jnp.dot(u_ref[b, :, blk * cin:(blk + 1) * cin], bblk_ref[blk],
                         preferred_element_type=F32)
            sre_ref[b, :, blk * half:(blk + 1) * half] = bu[:, :half]
            sim_ref[b, :, blk * half:(blk + 1) * half] = bu[:, half:]

    for j in range(ntiles):
        lanes = slice(j * LANES, (j + 1) * LANES)
        tabs = [tab_ref[i, :, lanes] for i in range(8)]

        def group(g, carry):
            rows = pl.ds(pl.multiple_of(g * SUBLANES, SUBLANES), SUBLANES)
            out = []
            for b in range(nb):
                xr, xi = sre_ref[b, rows, lanes], sim_ref[b, rows, lanes]
                for stage in range(3):
                    ar, ai = tabs[2 * stage], tabs[2 * stage + 1]
                    sr = pltpu.roll(xr, 1 << stage, axis=0)
                    si = pltpu.roll(xi, 1 << stage, axis=0)
                    xr, xi = xr + (ar * sr - ai * si), xi + (ar * si + ai * sr)
                cr, ci = carry[2 * b], carry[2 * b + 1]
                xr, xi = xr + (tabs[6] * cr - tabs[7] * ci), xi + (tabs[6] * ci + tabs[7] * cr)
                sre_ref[b, rows, lanes] = xr
                sim_ref[b, rows, lanes] = xi
                out.append(jnp.broadcast_to(xr[SUBLANES - 1:, :], xr.shape))
                out.append(jnp.broadcast_to(xi[SUBLANES - 1:, :], xi.shape))
            return tuple(out)

        init = tuple(state_ref[c, :, lanes] for c in range(2 * nb))
        final = lax.fori_loop(0, TC_S5 // SUBLANES, group, init)
        for c in range(2 * nb):
            state_ref[c, :, lanes] = final[c]

    for b in range(nb):
        for blk in range(S5_BLOCKS):
            sre = sre_ref[b, :, blk * half:(blk + 1) * half].astype(BF16)
            sim = sim_ref[b, :, blk * half:(blk + 1) * half].astype(BF16)
            y_ref[:, blk * cin:(blk + 1) * cin] = (
                jnp.dot(sre, cblk_ref[blk, :half, :], preferred_element_type=F32)
                + jnp.dot(sim, cblk_ref[blk, half:, :], preferred_element_type=F32))
        y = y_ref[...] + d_ref[...] * u_ref[b].astype(F32)
        y = jax.nn.gelu(y)
        glu = jnp.dot(y.astype(BF16), wglu_ref[...], preferred_element_type=F32)
        o_ref[b] = (y * _sigmoid(glu)).astype(BF16)


def _s5_params(lam_re, lam_im, b_re, b_im, c_re, c_im, log_step):
    g, p, h = S5_GROUPS, S5_STATE, S5_GROUP
    lr, li = lam_re.astype(F32), lam_im.astype(F32)
    step = jnp.exp(log_step.astype(F32))[:, None]
    mag = jnp.exp(lr * step)
    bar_re, bar_im = mag * jnp.cos(li * step), mag * jnp.sin(li * step)
    den = lr * lr + li * li
    f_re = ((bar_re - 1.0) * lr + bar_im * li) / den
    f_im = (bar_im * lr - (bar_re - 1.0) * li) / den
    f_re, f_im = f_re[:, :, None], f_im[:, :, None]
    bb_re = f_re * b_re.astype(F32) - f_im * b_im.astype(F32)
    bb_im = f_re * b_im.astype(F32) + f_im * b_re.astype(F32)
    gpb = g // S5_BLOCKS
    eye = jnp.eye(gpb, dtype=F32)

    def in_block(m):
        m = m.reshape(S5_BLOCKS, gpb, p, h)
        return jnp.einsum('kgph,gf->kghfp', m, eye).reshape(S5_BLOCKS, gpb * h, gpb * p)

    def out_block(m):
        m = m.reshape(S5_BLOCKS, gpb, h, p)
        return jnp.einsum('kghp,gf->kgpfh', m, eye).reshape(S5_BLOCKS, gpb * p, gpb * h)

    bblk = jnp.concatenate([in_block(bb_re), in_block(bb_im)], axis=2)
    cblk = jnp.concatenate([out_block(c_re.astype(F32)), out_block(-c_im.astype(F32))], axis=1)

    l_re, l_im = bar_re.reshape(1, g * p), bar_im.reshape(1, g * p)
    pows = [(l_re, l_im)]
    for _ in range(SUBLANES - 1):
        q_re, q_im = pows[-1]
        pows.append((q_re * l_re - q_im * l_im, q_re * l_im + q_im * l_re))
    sub = jnp.arange(SUBLANES)[:, None]
    rows = []
    for shift in (1, 2, 4):
        for part in pows[shift - 1]:
            rows.append(jnp.where(sub >= shift, part, 0.0))
    rows.append(jnp.concatenate([q[0] for q in pows], axis=0))
    rows.append(jnp.concatenate([q[1] for q in pows], axis=0))
    return bblk.astype(BF16), cblk.astype(BF16), jnp.stack(rows)


def _s5(u3, bblk, cblk, tab, d_row, wglu_bf):
    b, l, _ = u3.shape
    nstate = S5_GROUPS * S5_STATE
    const = lambda *shape: pl.BlockSpec(shape, lambda i: (0,) * len(shape))
    return pl.pallas_call(
        _s5_kernel,
        grid=(l // TC_S5,),
        in_specs=[
            pl.BlockSpec((b, TC_S5, S5_WIDTH), lambda i: (0, i, 0)),
            const(*bblk.shape), const(*cblk.shape), const(*tab.shape),
            const(1, S5_WIDTH), const(S5_WIDTH, S5_WIDTH),
        ],
        out_specs=pl.BlockSpec((b, TC_S5, S5_WIDTH), lambda i: (0, i, 0)),
        out_shape=jax.ShapeDtypeStruct((b, l, S5_WIDTH), BF16),
        scratch_shapes=[
            pltpu.VMEM((b, TC_S5, nstate), F32),
            pltpu.VMEM((b, TC_S5, nstate), F32),
            pltpu.VMEM((2 * b, SUBLANES, nstate), F32),
            pltpu.VMEM((TC_S5, S5_WIDTH), F32),
        ],
        compiler_params=pltpu.CompilerParams(
            dimension_semantics=("arbitrary",), vmem_limit_bytes=VMEM_LIMIT),
        name="s5_scan",
    )(u3, bblk, cblk, tab, d_row, wglu_bf)


def _merge_kernel(ysb_ref, ys5_ref, gate_ref, x_ref, wsb_ref, ws5_ref, wout_ref, g_ref,
                  wr_ref, br_ref, x1_ref, h2_ref, ek_ref, pk_ref, gk_ref, cnt_ref, carry_ref):
    @pl.when(pl.program_id(0) == 0)
    def _():
        carry_ref[...] = jnp.zeros_like(carry_ref)

    a = jnp.dot(ysb_ref[...], wsb_ref[...], preferred_element_type=F32)
    b = jnp.dot(ys5_ref[...], ws5_ref[...], preferred_element_type=F32)
    ga = _sigmoid(gate_ref[:, :D_MODEL].astype(F32))
    gb = _sigmoid(gate_ref[:, D_MODEL:].astype(F32))
    merged = (ga * a + gb * b).astype(BF16)
    x1 = x_ref[...] + jnp.dot(merged, wout_ref[...], preferred_element_type=F32)
    x1_ref[...] = x1
    h2 = _rmsnorm(x1, g_ref[...])
    h2_ref[...] = h2

    logits = lax.dot_general(wr_ref[...], h2, (((1,), (1,)), ((), ())),
                             precision=lax.Precision.HIGHEST,
                             preferred_element_type=F32) + br_ref[...]
    eidx = lax.broadcasted_iota(jnp.int32, logits.shape, 0)
    work = logits
    sel = jnp.zeros(logits.shape, jnp.bool_)
    top = None
    for k in range(TOP_K):
        m = jnp.max(work, axis=0, keepdims=True)
        first = jnp.min(jnp.where(work == m, eidx, N_EXPERTS), axis=0, keepdims=True)
        pick = eidx == first
        sel = jnp.logical_or(sel, pick)
        work = jnp.where(pick, -jnp.inf, work)
        if k == 0:
            top = m
    e = jnp.where(sel, jnp.exp(logits - top), 0.0)
    gate = e / jnp.sum(e, axis=0, keepdims=True)

    tm = logits.shape[1]
    mask = jnp.where(sel, 1.0, 0.0)
    trow = lax.broadcasted_iota(jnp.int32, (tm, tm), 0)
    tcol = lax.broadcasted_iota(jnp.int32, (tm, tm), 1)
    before = jnp.where(trow < tcol, 1.0, 0.0).astype(BF16)
    pos = jnp.dot(mask.astype(BF16), before, preferred_element_type=F32) + carry_ref[...]
    carry_ref[...] += jnp.sum(mask, axis=1, keepdims=True)
    cnt_ref[...] = jnp.broadcast_to(carry_ref[...], cnt_ref.shape)

    erow = lax.broadcasted_iota(jnp.int32, (N_EXPERTS, N_EXPERTS), 0)
    ecol = lax.broadcasted_iota(jnp.int32, (N_EXPERTS, N_EXPERTS), 1)
    lower = jnp.where(ecol < erow, 1.0, 0.0).astype(BF16)
    rank = jnp.dot(lower, mask.astype(BF16), preferred_element_type=F32)
    eidx_f = eidx.astype(F32)
    for k in range(TOP_K):
        ind = jnp.logical_and(sel, rank == float(k))
        pick1 = lambda v: jnp.sum(jnp.where(ind, v, 0.0), axis=0, keepdims=True)
        ek_ref[k:k + 1, :] = pick1(eidx_f).astype(jnp.int32)
        pk_ref[k:k + 1, :] = pick1(pos).astype(jnp.int32)
        gk_ref[k:k + 1, :] = pick1(gate)


def _merge(ysb, ys5, gates, x2, wsb, ws5, wout, g_ffn, wr_t, br_col):
    t = x2.shape[0]
    tm = TM_MERGE
    const = lambda *shape: pl.BlockSpec(shape, lambda i: (0,) * len(shape))
    return pl.pallas_call(
        _merge_kernel,
        grid=(t // tm,),
        in_specs=[
            pl.BlockSpec((tm, SB_WIDTH), lambda i: (i, 0)),
            pl.BlockSpec((tm, S5_WIDTH), lambda i: (i, 0)),
            pl.BlockSpec((tm, 2 * D_MODEL), lambda i: (i, 0)),
            pl.BlockSpec((tm, D_MODEL), lambda i: (i, 0)),
            const(SB_WIDTH, D_MODEL), const(S5_WIDTH, D_MODEL), const(D_MODEL, D_MODEL),
            const(1, D_MODEL), const(N_EXPERTS, D_MODEL), const(N_EXPERTS, 1),
        ],
        out_specs=[
            pl.BlockSpec((tm, D_MODEL), lambda i: (i, 0)),
            pl.BlockSpec((tm, D_MODEL), lambda i: (i, 0)),
            pl.BlockSpec((TOP_K, tm), lambda i: (0, i)),
            pl.BlockSpec((TOP_K, tm), lambda i: (0, i)),
            pl.BlockSpec((TOP_K, tm), lambda i: (0, i)),
            pl.BlockSpec((N_EXPERTS, LANES), lambda i: (0, 0)),
        ],
        out_shape=[
            jax.ShapeDtypeStruct((t, D_MODEL), F32),
            jax.ShapeDtypeStruct((t, D_MODEL), F32),
            jax.ShapeDtypeStruct((TOP_K, t), jnp.int32),
            jax.ShapeDtypeStruct((TOP_K, t), jnp.int32),
            jax.ShapeDtypeStruct((TOP_K, t), F32),
            jax.ShapeDtypeStruct((N_EXPERTS, LANES), F32),
        ],
        scratch_shapes=[pltpu.VMEM((N_EXPERTS, 1), F32)],
        compiler_params=pltpu.CompilerParams(
            dimension_semantics=("arbitrary",), vmem_limit_bytes=VMEM_LIMIT),
        name="merge_router",
    )(ysb, ys5, gates, x2, wsb, ws5, wout, g_ffn, wr_t, br_col)


def _moe_kernel(slot_ref, te_ref, nreal_ref, nused_ref, h2_hbm, wup_ref, bup_ref, wdn_ref, bdn_ref,
                yt_hbm, xbuf, ybuf, wup_bf, wdn_bf, rows_ref, gsem, ssem):
    r = pl.program_id(0)
    nused = nused_ref[0]
    t_total = h2_hbm.shape[0]

    def gather_start(tile, buf):
        def body(i, carry):
            tok = lax.rem(slot_ref[tile * TR + i], t_total)
            pltpu.make_async_copy(h2_hbm.at[pl.ds(tok, 1), :], xbuf.at[buf, pl.ds(i, 1), :],
                                  gsem.at[buf]).start()
            return carry
        lax.fori_loop(0, TR, body, 0)

    def gather_wait(buf):
        pltpu.make_async_copy(h2_hbm.at[pl.ds(0, TR), :], xbuf.at[buf], gsem.at[buf]).wait()

    def scatter_start(tile, buf):
        def body(i, carry):
            dst = slot_ref[tile * TR + i]
            pltpu.make_async_copy(ybuf.at[buf, pl.ds(i, 1), :], yt_hbm.at[pl.ds(dst, 1), :],
                                  ssem.at[buf]).start()
            return carry
        lax.fori_loop(0, nreal_ref[tile], body, 0)

    def scatter_wait(tile, buf):
        n = nreal_ref[tile]

        @pl.when(n > 0)
        def _():
            pltpu.make_async_copy(rows_ref.at[pl.ds(0, n)], rows_ref.at[pl.ds(0, n)],
                                  ssem.at[buf]).wait()

    @pl.when(r == 0)
    def _():
        gather_start(0, 0)

    @pl.when(r + 1 < nused)
    def _():
        gather_start(r + 1, lax.rem(r + 1, 2))

    @pl.when(r < nused)
    def _():
        buf = lax.rem(r, 2)
        gather_wait(buf)

        @pl.when(jnp.logical_or(r == 0, te_ref[r] != te_ref[jnp.maximum(r - 1, 0)]))
        def _():
            wup_bf[...] = wup_ref[0].astype(BF16)
            wdn_bf[...] = wdn_ref[0].astype(BF16)

        x = xbuf[buf].astype(BF16)
        hid = jnp.dot(x, wup_bf[...], preferred_element_type=F32) + bup_ref[0]
        glu = jnp.minimum(hid[:, :D_FF], SWIGLU_LIMIT)
        lin = jnp.clip(hid[:, D_FF:], -SWIGLU_LIMIT, SWIGLU_LIMIT)
        act = glu * _sigmoid(SWIGLU_ALPHA * glu) * (lin + 1.0)
        y = jnp.dot(act.astype(BF16), wdn_bf[...], preferred_element_type=F32) + bdn_ref[0]

        @pl.when(r >= 2)
        def _():
            scatter_wait(r - 2, buf)

        ybuf[buf] = y
        scatter_start(r, buf)

    @pl.when(jnp.logical_and(r == nused - 1, r >= 1))
    def _():
        scatter_wait(r - 1, lax.rem(r + 1, 2))

    @pl.when(r == nused - 1)
    def _():
        scatter_wait(r, lax.rem(r, 2))


def _moe(slot_of_row, tile_expert, tile_nreal, nused, h2, w_up, b_up, w_down, b_down):
    t = h2.shape[0]
    ntile = tile_expert.shape[0]
    by_expert = lambda r, slot, te, nreal, nu: (te[r], 0, 0)
    grid_spec = pltpu.PrefetchScalarGridSpec(
        num_scalar_prefetch=4,
        grid=(ntile,),
        in_specs=[
            pl.BlockSpec(memory_space=pl.ANY),
            pl.BlockSpec((1, D_MODEL, 2 * D_FF), by_expert),
            pl.BlockSpec((1, 1, 2 * D_FF), by_expert),
            pl.BlockSpec((1, D_FF, D_MODEL), by_expert),
            pl.BlockSpec((1, 1, D_MODEL), by_expert),
        ],
        out_specs=pl.BlockSpec(memory_space=pl.ANY),
        scratch_shapes=[
            pltpu.VMEM((2, TR, D_MODEL), F32),
            pltpu.VMEM((2, TR, D_MODEL), F32),
            pltpu.VMEM((D_MODEL, 2 * D_FF), BF16),
            pltpu.VMEM((D_FF, D_MODEL), BF16),
            pltpu.VMEM((TR, D_MODEL // LANES, LANES), F32),
            pltpu.SemaphoreType.DMA((2,)),
            pltpu.SemaphoreType.DMA((2,)),
        ],
    )
    return pl.pallas_call(
        _moe_kernel,
        grid_spec=grid_spec,
        out_shape=jax.ShapeDtypeStruct((TOP_K * t, D_MODEL), F32),
        compiler_params=pltpu.CompilerParams(
            dimension_semantics=("arbitrary",), vmem_limit_bytes=VMEM_LIMIT),
        name="moe_experts",
    )(slot_of_row, tile_expert, tile_nreal, nused, h2, w_up, b_up, w_down, b_down)


def _route(ek, pk, cnt, t):
    ntile = TOP_K * t // TR + N_EXPERTS
    counts = cnt[:, 0].astype(jnp.int32)
    padded = (counts + TR - 1) // TR * TR
    ends = jnp.cumsum(padded)
    starts = ends - padded
    dest = jnp.take(starts, ek) + pk
    slot = (jnp.arange(TOP_K, dtype=jnp.int32)[:, None] * t
            + jnp.arange(t, dtype=jnp.int32)[None, :])
    nrow = ntile * TR
    slot_of_row = jnp.zeros((nrow,), jnp.int32).at[dest.reshape(-1)].set(
        slot.reshape(-1), unique_indices=True)
    nused = ends[-1] // TR
    tile_start = jnp.arange(ntile, dtype=jnp.int32) * TR
    tile_expert = jnp.sum(tile_start[:, None] >= ends[None, :], axis=1).astype(jnp.int32)
    tile_expert = jnp.minimum(tile_expert, N_EXPERTS - 1)
    tile_nreal = jnp.clip(jnp.take(starts + counts, tile_expert) - tile_start, 0, TR)
    tile_nreal = jnp.where(tile_start < ends[-1], tile_nreal, 0).astype(jnp.int32)
    last = jnp.take(tile_expert, nused - 1)
    tile_expert = jnp.minimum(tile_expert, last)
    return slot_of_row, tile_expert, tile_nreal, nused.reshape(1).astype(jnp.int32)


def _combine_kernel(x1_ref, y0_ref, y1_ref, y2_ref, y3_ref, gk_ref, gfin_ref, o_ref):
    acc = x1_ref[...]
    for k, y_ref in enumerate((y0_ref, y1_ref, y2_ref, y3_ref)):
        acc = acc + gk_ref[k] * y_ref[...]
    o_ref[...] = _rmsnorm(acc, gfin_ref[...])


def _combine(x1, yt, gk_col, g_fin):
    t = x1.shape[0]
    tm = TM_COMBINE
    nblk = t // tm
    y_spec = lambda k: pl.BlockSpec((tm, D_MODEL), lambda i: (k * nblk + i, 0))
    return pl.pallas_call(
        _combine_kernel,
        grid=(nblk,),
        in_specs=[
            pl.BlockSpec((tm, D_MODEL), lambda i: (i, 0)),
            y_spec(0), y_spec(1), y_spec(2), y_spec(3),
            pl.BlockSpec((TOP_K, tm, 1), lambda i: (0, i, 0)),
            pl.BlockSpec((1, D_MODEL), lambda i: (0, 0)),
        ],
        out_specs=pl.BlockSpec((tm, D_MODEL), lambda i: (i, 0)),
        out_shape=jax.ShapeDtypeStruct((t, D_MODEL), F32),
        compiler_params=pltpu.CompilerParams(
            dimension_semantics=("parallel",), vmem_limit_bytes=VMEM_LIMIT),
        name="combine_norm",
    )(x1, yt, yt, yt, yt, gk_col, g_fin)


def _layout(ek, pk, cnt, t):
    ntile = TOP_K * t // TR + N_EXPERTS
    counts = cnt[:, 0].astype(jnp.int32)
    padded = (counts + TR - 1) // TR * TR
    ends = jnp.cumsum(padded)
    starts = ends - padded
    start_of = jnp.zeros_like(ek)
    for e in range(N_EXPERTS):
        start_of = jnp.where(ek == e, starts[e], start_of)
    dest = (start_of + pk).reshape(-1)
    nused = ends[-1] // TR
    tile_start = jnp.arange(ntile, dtype=jnp.int32) * TR
    tile_expert = jnp.sum(tile_start[:, None] >= ends[None, :], axis=1).astype(jnp.int32)
    tile_expert = jnp.minimum(tile_expert, jnp.take(tile_expert, nused - 1))
    tail_tile = jnp.maximum(ends // TR - 1, 0).astype(jnp.int32)
    return dest, counts, tail_tile, tile_expert, nused.reshape(1).astype(jnp.int32)


def _dispatch_kernel(dest_ref, cnt_ref, tail_ref, nused_ref, h2_ref, xg_hbm, zero_ref, sem, zsem):
    i = pl.program_id(0)
    tm, t_total = h2_ref.shape[0], dest_ref.shape[0] // TOP_K

    @pl.when(i == 0)
    def _():
        zero_ref[...] = jnp.zeros_like(zero_ref)
        fill = lambda e: pltpu.make_async_copy(
            zero_ref, xg_hbm.at[pl.ds(pl.multiple_of(tail_ref[e] * TR, TR), TR), :], zsem)
        for e in range(N_EXPERTS):
            @pl.when(cnt_ref[e] > 0)
            def _():
                fill(e).start()
        for e in range(N_EXPERTS):
            @pl.when(cnt_ref[e] > 0)
            def _():
                fill(e).wait()

        spare = lambda r: pltpu.make_async_copy(
            zero_ref, xg_hbm.at[pl.ds(pl.multiple_of(r * TR, TR), TR), :], zsem)
        ntile = xg_hbm.shape[0] // TR
        lax.fori_loop(nused_ref[0], ntile, lambda r, c: (spare(r).start(), c)[1], 0)
        lax.fori_loop(nused_ref[0], ntile, lambda r, c: (spare(r).wait(), c)[1], 0)

    def body(j, carry):
        for k in range(TOP_K):
            dst = dest_ref[k * t_total + i * tm + j]
            pltpu.make_async_copy(h2_ref.at[pl.ds(j, 1), :], xg_hbm.at[pl.ds(dst, 1), :], sem).start()
        return carry

    lax.fori_loop(0, tm, body, 0, unroll=8)
    pltpu.make_async_copy(xg_hbm.at[pl.ds(TOP_K * tm, TOP_K * tm), :],
                          xg_hbm.at[pl.ds(0, TOP_K * tm), :], sem).wait()


def _dispatch(dest, counts, tail_tile, nused, h2, nrow):
    t = h2.shape[0]
    tm = TM_DISPATCH
    grid_spec = pltpu.PrefetchScalarGridSpec(
        num_scalar_prefetch=4,
        grid=(t // tm,),
        in_specs=[pl.BlockSpec((tm, D_MODEL), lambda i, *_: (i, 0))],
        out_specs=pl.BlockSpec(memory_space=pl.ANY),
        scratch_shapes=[pltpu.VMEM((TR, D_MODEL), F32), pltpu.SemaphoreType.DMA(()),
                        pltpu.SemaphoreType.DMA(())],
    )
    return pl.pallas_call(
        _dispatch_kernel,
        grid_spec=grid_spec,
        out_shape=jax.ShapeDtypeStruct((nrow, D_MODEL), F32),
        compiler_params=pltpu.CompilerParams(
            dimension_semantics=("arbitrary",), vmem_limit_bytes=VMEM_LIMIT),
        name="moe_dispatch",
    )(dest, counts, tail_tile, nused, h2)


def _experts_kernel(te_ref, nused_ref, xg_ref, wup_ref, bup_ref, wdn_ref, bdn_ref, y_ref,
                    wup_bf, wdn_bf):
    r = pl.program_id(0)

    @pl.when(r < nused_ref[0])
    def _():
        @pl.when(jnp.logical_or(r == 0, te_ref[r] != te_ref[jnp.maximum(r - 1, 0)]))
        def _():
            wup_bf[...] = wup_ref[0].astype(BF16)
            wdn_bf[...] = wdn_ref[0].astype(BF16)

        x = xg_ref[...].astype(BF16)
        hid = jnp.dot(x, wup_bf[...], preferred_element_type=F32) + bup_ref[0]
        glu = jnp.minimum(hid[:, :D_FF], SWIGLU_LIMIT)
        lin = jnp.clip(hid[:, D_FF:], -SWIGLU_LIMIT, SWIGLU_LIMIT)
        act = glu * _sigmoid(SWIGLU_ALPHA * glu) * (lin + 1.0)
        y_ref[...] = jnp.dot(act.astype(BF16), wdn_bf[...], preferred_element_type=F32) + bdn_ref[0]

    @pl.when(r >= nused_ref[0])
    def _():
        y_ref[...] = jnp.zeros_like(y_ref)


def _experts(tile_expert, nused, xg, w_up, b_up, w_down, b_down):
    ntile = tile_expert.shape[0]
    by_expert = lambda r, te, nu: (te[r], 0, 0)
    by_block = lambda r, te, nu: (r, 0)
    grid_spec = pltpu.PrefetchScalarGridSpec(
        num_scalar_prefetch=2,
        grid=(ntile,),
        in_specs=[
            pl.BlockSpec((TR, D_MODEL), by_block),
            pl.BlockSpec((1, D_MODEL, 2 * D_FF), by_expert),
            pl.BlockSpec((1, 1, 2 * D_FF), by_expert),
            pl.BlockSpec((1, D_FF, D_MODEL), by_expert),
            pl.BlockSpec((1, 1, D_MODEL), by_expert),
        ],
        out_specs=pl.BlockSpec((TR, D_MODEL), by_block),
        scratch_shapes=[pltpu.VMEM((D_MODEL, 2 * D_FF), BF16), pltpu.VMEM((D_FF, D_MODEL), BF16)],
    )
    return pl.pallas_call(
        _experts_kernel,
        grid_spec=grid_spec,
        out_shape=jax.ShapeDtypeStruct(xg.shape, F32),
        compiler_params=pltpu.CompilerParams(
            dimension_semantics=("arbitrary",), vmem_limit_bytes=VMEM_LIMIT),
        name="moe_experts",
    )(tile_expert, nused, xg, w_up, b_up, w_down, b_down)


def _gather_combine_kernel(dest_ref, x1_ref, gk_ref, gfin_ref, y_hbm, o_ref, ybuf, sem):
    i = pl.program_id(0)
    n = pl.num_programs(0)
    tm, t_total = x1_ref.shape[0], dest_ref.shape[0] // TOP_K

    def gather_start(tile, buf):
        def body(j, carry):
            for k in range(TOP_K):
                src = dest_ref[k * t_total + tile * tm + j]
                pltpu.make_async_copy(y_hbm.at[pl.ds(src, 1), :], ybuf.at[buf, k, pl.ds(j, 1), :],
                                      sem.at[buf]).start()
            return carry
        lax.fori_loop(0, tm, body, 0, unroll=8)

    @pl.when(i == 0)
    def _():
        gather_start(0, 0)

    @pl.when(i + 1 < n)
    def _():
        gather_start(i + 1, lax.rem(i + 1, 2))

    buf = lax.rem(i, 2)
    pltpu.make_async_copy(ybuf.at[1 - buf], ybuf.at[buf], sem.at[buf]).wait()
    acc = x1_ref[...]
    for k in range(TOP_K):
        acc = acc + gk_ref[k] * ybuf[buf, k]
    o_ref[...] = _rmsnorm(acc, gfin_ref[...])


def _gather_combine(dest, x1, gk_col, g_fin, y):
    t = x1.shape[0]
    tm = TM_COMBINE
    grid_spec = pltpu.PrefetchScalarGridSpec(
        num_scalar_prefetch=1,
        grid=(t // tm,),
        in_specs=[
            pl.BlockSpec((tm, D_MODEL), lambda i, d: (i, 0)),
            pl.BlockSpec((TOP_K, tm, 1), lambda i, d: (0, i, 0)),
            pl.BlockSpec((1, D_MODEL), lambda i, d: (0, 0)),
            pl.BlockSpec(memory_space=pl.ANY),
        ],
        out_specs=pl.BlockSpec((tm, D_MODEL), lambda i, d: (i, 0)),
        scratch_shapes=[pltpu.VMEM((2, TOP_K, tm, D_MODEL), F32), pltpu.SemaphoreType.DMA((2,))],
    )
    return pl.pallas_call(
        _gather_combine_kernel,
        grid_spec=grid_spec,
        out_shape=jax.ShapeDtypeStruct((t, D_MODEL), F32),
        compiler_params=pltpu.CompilerParams(
            dimension_semantics=("arbitrary",), vmem_limit_bytes=VMEM_LIMIT),
        name="gather_combine_norm",
    )(dest, x1, gk_col, g_fin, y)


def kernel(x, norm_mix, w_in, s5_lam_re, s5_lam_im, s5_b_re, s5_b_im, s5_c_re, s5_c_im, s5_d,
           s5_log_step, w_glu, w_branch_sb, w_branch_s5, w_out, norm_ffn, w_router, b_router,
           w_up, b_up, w_down, b_down, norm_final):
    bsz, seq, d = x.shape
    t = bsz * seq
    assert norm_mix.shape[0] == 1, "single-layer block"
    x2 = x.reshape(t, d).astype(F32)
    for l in range(1):
        qkv, u, gates = _inproj(x2, norm_mix[l].reshape(1, d).astype(F32), w_in[l].astype(BF16))
        ysb = _attention(qkv.reshape(bsz, seq, 3 * SB_WIDTH))
        bblk, cblk, tab = _s5_params(s5_lam_re[l], s5_lam_im[l], s5_b_re[l], s5_b_im[l],
                                     s5_c_re[l], s5_c_im[l], s5_log_step[l])
        ys5 = _s5(u.reshape(bsz, seq, S5_WIDTH), bblk, cblk, tab,
                  s5_d[l].reshape(1, S5_WIDTH).astype(F32), w_glu[l].astype(BF16))
        x1, h2, ek, pk, gk, cnt = _merge(
            ysb.reshape(t, SB_WIDTH), ys5.reshape(t, S5_WIDTH), gates, x2,
            w_branch_sb[l].astype(BF16), w_branch_s5[l].astype(BF16), w_out[l].astype(BF16),
            norm_ffn[l].reshape(1, d).astype(F32), w_router[l].T.astype(F32),
            b_router[l].reshape(N_EXPERTS, 1).astype(F32))
        dest, counts, tail_tile, tile_expert, nused = _layout(ek, pk, cnt, t)
        xg = _dispatch(dest, counts, tail_tile, nused, h2, tile_expert.shape[0] * TR)
        y = _experts(tile_expert, nused, xg, w_up[l],
                     b_up[l].reshape(N_EXPERTS, 1, 2 * D_FF), w_down[l],
                     b_down[l].reshape(N_EXPERTS, 1, D_MODEL))
        x2 = _gather_combine(dest, x1, gk.reshape(TOP_K, t, 1),
                             norm_final.reshape(1, d).astype(F32), y)
    return x2.reshape(bsz, seq, d).astype(x.dtype)
```

```python
import jax
import jax.numpy as jnp
from jax import lax
from jax.experimental import pallas as pl
from jax.experimental.pallas import tpu as pltpu

F32 = jnp.float32
BF16 = jnp.bfloat16

D_MODEL = 1024
SB_HEADS = 8
SB_HEAD_DIM = 64
SB_WIDTH = SB_HEADS * SB_HEAD_DIM
S5_WIDTH = 512
S5_GROUP = 16
S5_GROUPS = 32
S5_STATE = 64
N_EXPERTS = 32
TOP_K = 4
D_FF = 1024
SWIGLU_LIMIT = 7.0
SWIGLU_ALPHA = 1.702
RMS_EPS = 1e-5
OFF_U = 3 * SB_WIDTH
OFF_GATES = OFF_U + S5_WIDTH
IN_WIDTH = OFF_GATES + 2 * D_MODEL

LANES = 128
SUBLANES = 8
VMEM_LIMIT = 56 * 1024 * 1024

TM_PROJ = 512
TQ = 256
TK = 256
EXP_UNDERFLOW = 110.0
TC_S5 = 256
S5_BLOCKS = 4
TM_MERGE = 512
SUB_MERGE = 256
TR = 512
TM_COMBINE = 256
TM_DISPATCH = 256


def _rmsnorm(x, g):
    return x * lax.rsqrt(jnp.mean(x * x, axis=-1, keepdims=True) + RMS_EPS) * g


def _sigmoid(x):
    return 1.0 / (1.0 + jnp.exp(-x))


def _inproj_kernel(x_ref, g_ref, w_ref, qkv_ref, u_ref, gate_ref):
    hb = _rmsnorm(x_ref[...], g_ref[...]).astype(BF16)
    chunk = SB_WIDTH

    def proj(c0):
        return jnp.dot(hb, w_ref[:, c0:c0 + chunk], preferred_element_type=F32)

    qkv_ref[:, 0:chunk] = (proj(0) * (SB_HEAD_DIM ** -0.5)).astype(BF16)
    qkv_ref[:, chunk:2 * chunk] = proj(chunk).astype(BF16)
    qkv_ref[:, 2 * chunk:3 * chunk] = proj(2 * chunk).astype(BF16)
    u_ref[...] = proj(OFF_U).astype(BF16)
    for c in range(2 * D_MODEL // chunk):
        gate_ref[:, c * chunk:(c + 1) * chunk] = proj(OFF_GATES + c * chunk).astype(BF16)


def _inproj(x2, g, w_bf):
    t = x2.shape[0]
    return pl.pallas_call(
        _inproj_kernel,
        grid=(t // TM_PROJ,),
        in_specs=[
            pl.BlockSpec((TM_PROJ, D_MODEL), lambda i: (i, 0)),
            pl.BlockSpec((1, D_MODEL), lambda i: (0, 0)),
            pl.BlockSpec((D_MODEL, IN_WIDTH), lambda i: (0, 0)),
        ],
        out_specs=[
            pl.BlockSpec((TM_PROJ, 3 * SB_WIDTH), lambda i: (i, 0)),
            pl.BlockSpec((TM_PROJ, S5_WIDTH), lambda i: (i, 0)),
            pl.BlockSpec((TM_PROJ, 2 * D_MODEL), lambda i: (i, 0)),
        ],
        out_shape=[
            jax.ShapeDtypeStruct((t, 3 * SB_WIDTH), BF16),
            jax.ShapeDtypeStruct((t, S5_WIDTH), BF16),
            jax.ShapeDtypeStruct((t, 2 * D_MODEL), BF16),
        ],
        compiler_params=pltpu.CompilerParams(
            dimension_semantics=("parallel",), vmem_limit_bytes=VMEM_LIMIT),
        name="inproj",
    )(x2, g, w_bf)


def _attn_kernel(q_ref, k_ref, v_ref, o_ref, acc_ref, cb0_ref, cb1_ref):
    qi = pl.program_id(2)
    q = q_ref[0]
    is_h0 = lax.broadcasted_iota(jnp.int32, (TQ, LANES), 1) < SB_HEAD_DIM
    row = lax.broadcasted_iota(jnp.int32, (TQ, TK), 0)
    col = lax.broadcasted_iota(jnp.int32, (TQ, TK), 1)
    tri = jnp.where(row >= col, 1.0, 0.0).astype(BF16)
    causal = col < row
    zero = jnp.zeros_like(q)
    qms = (jnp.where(is_h0, q, zero), jnp.where(is_h0, zero, q))
    cbs = (cb0_ref, cb1_ref)
    acc_ref[...] = jnp.zeros_like(acc_ref)
    cb0_ref[...] = jnp.zeros_like(cb0_ref)
    cb1_ref[...] = jnp.zeros_like(cb1_ref)

    def scores(qm, k, mask):
        z = lax.dot_general(qm, k, (((1,), (1,)), ((), ())), preferred_element_type=F32)
        log_keep = -(jnp.maximum(z, 0.0) + jnp.log(1.0 + jnp.exp(-jnp.abs(z))))
        if mask is not None:
            log_keep = jnp.where(mask, log_keep, 0.0)
        hi = log_keep.astype(BF16)
        lo = (log_keep - hi.astype(F32)).astype(BF16)
        rcum = (jnp.dot(hi, tri, preferred_element_type=F32)
                + jnp.dot(lo, tri, preferred_element_type=F32))
        return z, rcum

    def pair(kt, diag):
        has_b = kt >= 1
        ka = pl.multiple_of(kt * TK, TK)
        kb = pl.multiple_of(jnp.maximum(kt - 1, 0) * TK, TK)
        k_a, v_a = k_ref[0, pl.ds(ka, TK), :], v_ref[0, pl.ds(ka, TK), :]
        k_b, v_b = k_ref[0, pl.ds(kb, TK), :], v_ref[0, pl.ds(kb, TK), :]
        mask_a = causal if diag else None
        pvs = []
        for qm, cb_ref in zip(qms, cbs):
            z_a, rcum_a = scores(qm, k_a, mask_a)
            z_b, rcum_b = scores(qm, k_b, has_b)
            cb = cb_ref[...]
            cb_a = cb + rcum_a[:, 0:1]
            w_a = jnp.exp(z_a + rcum_a + cb)
            if diag:
                w_a = jnp.where(causal, w_a, 0.0)
            w_b = jnp.where(has_b, jnp.exp(z_b + rcum_b + cb_a), 0.0)
            pvs.append(jnp.dot(w_a.astype(BF16), v_a, preferred_element_type=F32)
                       + jnp.dot(w_b.astype(BF16), v_b, preferred_element_type=F32))
            cb_ref[...] = cb_a + rcum_b[:, 0:1]
        acc_ref[...] += jnp.where(is_h0, pvs[0], pvs[1])
        worst = jnp.maximum(jnp.max(cb0_ref[...]), jnp.max(cb1_ref[...]))
        return jnp.logical_not(worst < -EXP_UNDERFLOW)

    live = pair(qi, True)

    def cond(carry):
        kt, live = carry
        return jnp.logical_and(kt >= 0, live)

    def body(carry):
        kt, _ = carry
        return kt - 2, pair(kt, False)

    lax.while_loop(cond, body, (qi - 2, live))
    o_ref[0] = acc_ref[...].astype(BF16)


def _attention(qkv3):
    b, l, _ = qkv3.shape
    pairs = SB_WIDTH // LANES
    return pl.pallas_call(
        _attn_kernel,
        grid=(b, pairs, l // TQ),
        in_specs=[
            pl.BlockSpec((1, TQ, LANES), lambda bi, hp, qi: (bi, qi, hp)),
            pl.BlockSpec((1, l, LANES), lambda bi, hp, qi: (bi, 0, pairs + hp)),
            pl.BlockSpec((1, l, LANES), lambda bi, hp, qi: (bi, 0, 2 * pairs + hp)),
        ],
        out_specs=pl.BlockSpec((1, TQ, LANES), lambda bi, hp, qi: (bi, qi, hp)),
        out_shape=jax.ShapeDtypeStruct((b, l, SB_WIDTH), BF16),
        scratch_shapes=[pltpu.VMEM((TQ, LANES), F32), pltpu.VMEM((TQ, 1), F32),
                        pltpu.VMEM((TQ, 1), F32)],
        compiler_params=pltpu.CompilerParams(
            dimension_semantics=("parallel", "parallel", "arbitrary"),
            vmem_limit_bytes=VMEM_LIMIT),
        name="sb_attention",
    )(qkv3, qkv3, qkv3)


def _s5_kernel(u_ref, bblk_ref, cblk_ref, tab_ref, d_ref, wglu_ref, o_ref,
               sre_ref, sim_ref, state_ref, y_ref):
    nb = u_ref.shape[0]
    half = S5_GROUPS * S5_STATE // S5_BLOCKS
    cin = S5_WIDTH // S5_BLOCKS
    ntiles = S5_GROUPS * S5_STATE // LANES

    @pl.when(pl.program_id(0) == 0)
    def _():
        state_ref[...] = jnp.zeros_like(state_ref)

    for b in range(nb):
        for blk in range(S5_BLOCKS):
            bu = jnp.dot(u_ref[b, :, blk * cin:(blk + 1) * cin], bblk_ref[blk],
                         preferred_element_type=F32)
            sre_ref[b, :, blk * half:(blk + 1) * half] = bu[:, :half]
            sim_ref[b, :, blk * half:(blk + 1) * half] = bu[:, half:]

    for j in range(ntiles):
        lanes = slice(j * LANES, (j + 1) * LANES)
        tabs = [tab_ref[i, :, lanes] for i in range(8)]

        def group(g, carry):
            rows = pl.ds(pl.multiple_of(g * SUBLANES, SUBLANES), SUBLANES)
            out = []
            for b in range(nb):
                xr, xi = sre_ref[b, rows, lanes], sim_ref[b, rows, lanes]
                for stage in range(3):
                    ar, ai = tabs[2 * stage], tabs[2 * stage + 1]
                    sr = pltpu.roll(xr, 1 << stage, axis=0)
                    si = pltpu.roll(xi, 1 << stage, axis=0)
                    xr, xi = xr + (ar * sr - ai * si), xi + (ar * si + ai * sr)
                cr, ci = carry[2 * b], carry[2 * b + 1]
                xr, xi = xr + (tabs[6] * cr - tabs[7] * ci), xi + (tabs[6] * ci + tabs[7] * cr)
                sre_ref[b, rows, lanes] = xr
                sim_ref[b, rows, lanes] = xi
                out.append(jnp.broadcast_to(xr[SUBLANES - 1:, :], xr.shape))
                out.append(jnp.broadcast_to(xi[SUBLANES - 1:, :], xi.shape))
            return tuple(out)

        init = tuple(state_ref[c, :, lanes] for c in range(2 * nb))
        final = lax.fori_loop(0, TC_S5 // SUBLANES, group, init)
        for c in range(2 * nb):
            state_ref[c, :, lanes] = final[c]

    for b in range(nb):
        for blk in range(S5_BLOCKS):
            sre = sre_ref[b, :, blk * half:(blk + 1) * half].astype(BF16)
            sim = sim_ref[b, :, blk * half:(blk + 1) * half].astype(BF16)
            y_ref[:, blk * cin:(blk + 1) * cin] = (
                jnp.dot(sre, cblk_ref[blk, :half, :], preferred_element_type=F32)
                + jnp.dot(sim, cblk_ref[blk, half:, :], preferred_element_type=F32))
        y = y_ref[...] + d_ref[...] * u_ref[b].astype(F32)
        y = jax.nn.gelu(y)
        glu = jnp.dot(y.astype(BF16), wglu_ref[...], preferred_element_type=F32)
        o_ref[b] = (y * _sigmoid(glu)).astype(BF16)


def _s5_params(lam_re, lam_im, b_re, b_im, c_re, c_im, log_step):
    g, p, h = S5_GROUPS, S5_STATE, S5_GROUP
    lr, li = lam_re.astype(F32), lam_im.astype(F32)
    step = jnp.exp(log_step.astype(F32))[:, None]
    mag = jnp.exp(lr * step)
    bar_re, bar_im = mag * jnp.cos(li * step), mag * jnp.sin(li * step)
    den = lr * lr + li * li
    f_re = ((bar_re - 1.0) * lr + bar_im * li) / den
    f_im = (bar_im * lr - (bar_re - 1.0) * li) / den
    f_re, f_im = f_re[:, :, None], f_im[:, :, None]
    bb_re = f_re * b_re.astype(F32) - f_im * b_im.astype(F32)
    bb_im = f_re * b_im.astype(F32) + f_im * b_re.astype(F32)
    gpb = g // S5_BLOCKS
    eye = jnp.eye(gpb, dtype=F32)

    def in_block(m):
        m = m.reshape(S5_BLOCKS, gpb, p, h)
        return jnp.einsum('kgph,gf->kghfp', m, eye).reshape(S5_BLOCKS, gpb * h, gpb * p)

    def out_block(m):
        m = m.reshape(S5_BLOCKS, gpb, h, p)
        return jnp.einsum('kghp,gf->kgpfh', m, eye).reshape(S5_BLOCKS, gpb * p, gpb * h)

    bblk = jnp.concatenate([in_block(bb_re), in_block(bb_im)], axis=2)
    cblk = jnp.concatenate([out_block(c_re.astype(F32)), out_block(-c_im.astype(F32))], axis=1)

    l_re, l_im = bar_re.reshape(1, g * p), bar_im.reshape(1, g * p)
    pows = [(l_re, l_im)]
    for _ in range(SUBLANES - 1):
        q_re, q_im = pows[-1]
        pows.append((q_re * l_re - q_im * l_im, q_re * l_im + q_im * l_re))
    sub = jnp.arange(SUBLANES)[:, None]
    rows = []
    for shift in (1, 2, 4):
        for part in pows[shift - 1]:
            rows.append(jnp.where(sub >= shift, part, 0.0))
    rows.append(jnp.concatenate([q[0] for q in pows], axis=0))
    rows.append(jnp.concatenate([q[1] for q in pows], axis=0))
    return bblk.astype(BF16), cblk.astype(BF16), jnp.stack(rows)


def _s5(u3, bblk, cblk, tab, d_row, wglu_bf):
    b, l, _ = u3.shape
    nstate = S5_GROUPS * S5_STATE
    const = lambda *shape: pl.BlockSpec(shape, lambda i: (0,) * len(shape))
    return pl.pallas_call(
        _s5_kernel,
        grid=(l // TC_S5,),
        in_specs=[
            pl.BlockSpec((b, TC_S5, S5_WIDTH), lambda i: (0, i, 0)),
            const(*bblk.shape), const(*cblk.shape), const(*tab.shape),
            const(1, S5_WIDTH), const(S5_WIDTH, S5_WIDTH),
        ],
        out_specs=pl.BlockSpec((b, TC_S5, S5_WIDTH), lambda i: (0, i, 0)),
        out_shape=jax.ShapeDtypeStruct((b, l, S5_WIDTH), BF16),
        scratch_shapes=[
            pltpu.VMEM((b, TC_S5, nstate), F32),
            pltpu.VMEM((b, TC_S5, nstate), F32),
            pltpu.VMEM((2 * b, SUBLANES, nstate), F32),
            pltpu.VMEM((TC_S5, S5_WIDTH), F32),
        ],
        compiler_params=pltpu.CompilerParams(
            dimension_semantics=("arbitrary",), vmem_limit_bytes=VMEM_LIMIT),
        name="s5_scan",
    )(u3, bblk, cblk, tab, d_row, wglu_bf)


def _merge_kernel(ysb_ref, ys5_ref, gate_ref, x_ref, wsb_ref, ws5_ref, wout_ref, g_ref,
                  wr_ref, br_ref, x1_ref, h2_ref, ek_ref, pk_ref, gk_ref, cnt_ref, carry_ref):
    @pl.when(pl.program_id(0) == 0)
    def _():
        carry_ref[...] = jnp.zeros_like(carry_ref)

    sub = SUB_MERGE
    trow = lax.broadcasted_iota(jnp.int32, (sub, sub), 0)
    tcol = lax.broadcasted_iota(jnp.int32, (sub, sub), 1)
    before = jnp.where(trow < tcol, 1.0, 0.0).astype(BF16)
    erow = lax.broadcasted_iota(jnp.int32, (N_EXPERTS, N_EXPERTS), 0)
    ecol = lax.broadcasted_iota(jnp.int32, (N_EXPERTS, N_EXPERTS), 1)
    lower = jnp.where(ecol < erow, 1.0, 0.0).astype(BF16)
    eidx = lax.broadcasted_iota(jnp.int32, (N_EXPERTS, sub), 0)
    eidx_f = eidx.astype(F32)

    for s in range(x_ref.shape[0] // sub):
        rows = slice(s * sub, (s + 1) * sub)
        a = jnp.dot(ysb_ref[rows, :], wsb_ref[...], preferred_element_type=F32)
        b = jnp.dot(ys5_ref[rows, :], ws5_ref[...], preferred_element_type=F32)
        ga = _sigmoid(gate_ref[rows, :D_MODEL].astype(F32))
        gb = _sigmoid(gate_ref[rows, D_MODEL:].astype(F32))
        merged = (ga * a + gb * b).astype(BF16)
        x1 = x_ref[rows, :] + jnp.dot(merged, wout_ref[...], preferred_element_type=F32)
        x1_ref[rows, :] = x1
        h2 = _rmsnorm(x1, g_ref[...])
        h2_ref[rows, :] = h2

        logits = lax.dot_general(wr_ref[...], h2, (((1,), (1,)), ((), ())),
                                 precision=lax.Precision.HIGHEST,
                                 preferred_element_type=F32) + br_ref[...]
        work = logits
        sel = jnp.zeros(logits.shape, jnp.bool_)
        top = None
        for k in range(TOP_K):
            m = jnp.max(work, axis=0, keepdims=True)
            first = jnp.min(jnp.where(work == m, eidx, N_EXPERTS), axis=0, keepdims=True)
            pick = eidx == first
            sel = jnp.logical_or(sel, pick)
            work = jnp.where(pick, -jnp.inf, work)
            if k == 0:
                top = m
        e = jnp.where(sel, jnp.exp(logits - top), 0.0)
        gate = e / jnp.sum(e, axis=0, keepdims=True)

        mask = jnp.where(sel, 1.0, 0.0)
        pos = jnp.dot(mask.astype(BF16), before, preferred_element_type=F32) + carry_ref[...]
        carry_ref[...] += jnp.sum(mask, axis=1, keepdims=True)

        rank = jnp.dot(lower, mask.astype(BF16), preferred_element_type=F32)
        for k in range(TOP_K):
            ind = jnp.logical_and(sel, rank == float(k))
            pick1 = lambda v: jnp.sum(jnp.where(ind, v, 0.0), axis=0, keepdims=True)
            ek_ref[k:k + 1, rows] = pick1(eidx_f).astype(jnp.int32)
            pk_ref[k:k + 1, rows] = pick1(pos).astype(jnp.int32)
            gk_ref[k:k + 1, rows] = pick1(gate)

    cnt_ref[...] = jnp.broadcast_to(carry_ref[...], cnt_ref.shape)


def _merge(ysb, ys5, gates, x2, wsb, ws5, wout, g_ffn, wr_t, br_col):
    t = x2.shape[0]
    tm = TM_MERGE
    const = lambda *shape: pl.BlockSpec(shape, lambda i: (0,) * len(shape))
    return pl.pallas_call(
        _merge_kernel,
        grid=(t // tm,),
        in_specs=[
            pl.BlockSpec((tm, SB_WIDTH), lambda i: (i, 0)),
            pl.BlockSpec((tm, S5_WIDTH), lambda i: (i, 0)),
            pl.BlockSpec((tm, 2 * D_MODEL), lambda i: (i, 0)),
            pl.BlockSpec((tm, D_MODEL), lambda i: (i, 0)),
            const(SB_WIDTH, D_MODEL), const(S5_WIDTH, D_MODEL), const(D_MODEL, D_MODEL),
            const(1, D_MODEL), const(N_EXPERTS, D_MODEL), const(N_EXPERTS, 1),
        ],
        out_specs=[
            pl.BlockSpec((tm, D_MODEL), lambda i: (i, 0)),
            pl.BlockSpec((tm, D_MODEL), lambda i: (i, 0)),
            pl.BlockSpec((TOP_K, tm), lambda i: (0, i)),
            pl.BlockSpec((TOP_K, tm), lambda i: (0, i)),
            pl.BlockSpec((TOP_K, tm), lambda i: (0, i)),
            pl.BlockSpec((N_EXPERTS, LANES), lambda i: (0, 0)),
        ],
        out_shape=[
            jax.ShapeDtypeStruct((t, D_MODEL), F32),
            jax.ShapeDtypeStruct((t, D_MODEL), F32),
            jax.ShapeDtypeStruct((TOP_K, t), jnp.int32),
            jax.ShapeDtypeStruct((TOP_K, t), jnp.int32),
            jax.ShapeDtypeStruct((TOP_K, t), F32),
            jax.ShapeDtypeStruct((N_EXPERTS, LANES), F32),
        ],
        scratch_shapes=[pltpu.VMEM((N_EXPERTS, 1), F32)],
        compiler_params=pltpu.CompilerParams(
            dimension_semantics=("arbitrary",), vmem_limit_bytes=VMEM_LIMIT),
        name="merge_router",
    )(ysb, ys5, gates, x2, wsb, ws5, wout, g_ffn, wr_t, br_col)


def _layout(ek, pk, cnt, t):
    ntile = TOP_K * t // TR + N_EXPERTS
    counts = cnt[:, 0].astype(jnp.int32)
    padded = (counts + TR - 1) // TR * TR
    ends = jnp.cumsum(padded)
    starts = ends - padded
    start_of = jnp.zeros_like(ek)
    for e in range(N_EXPERTS):
        start_of = jnp.where(ek == e, starts[e], start_of)
    dest = (start_of + pk).reshape(-1)
    nused = ends[-1] // TR
    tile_start = jnp.arange(ntile, dtype=jnp.int32) * TR
    tile_expert = jnp.sum(tile_start[:, None] >= ends[None, :], axis=1).astype(jnp.int32)
    tile_expert = jnp.minimum(tile_expert, jnp.take(tile_expert, nused - 1))
    tail_tile = jnp.maximum(ends // TR - 1, 0).astype(jnp.int32)
    return dest, counts, tail_tile, tile_expert, nused.reshape(1).astype(jnp.int32)


def _dispatch_kernel(dest_ref, cnt_ref, tail_ref, nused_ref, h2_ref, xg_hbm, zero_ref, sem, zsem):
    i = pl.program_id(0)
    tm, t_total = h2_ref.shape[0], dest_ref.shape[0] // TOP_K

    @pl.when(i == 0)
    def _():
        zero_ref[...] = jnp.zeros_like(zero_ref)
        fill = lambda e: pltpu.make_async_copy(
            zero_ref, xg_hbm.at[pl.ds(pl.multiple_of(tail_ref[e] * TR, TR), TR), :], zsem)
        for e in range(N_EXPERTS):
            @pl.when(cnt_ref[e] > 0)
            def _():
                fill(e).start()
        for e in range(N_EXPERTS):
            @pl.when(cnt_ref[e] > 0)
            def _():
                fill(e).wait()

        spare = lambda r: pltpu.make_async_copy(
            zero_ref, xg_hbm.at[pl.ds(pl.multiple_of(r * TR, TR), TR), :], zsem)
        ntile = xg_hbm.shape[0] // TR
        lax.fori_loop(nused_ref[0], ntile, lambda r, c: (spare(r).start(), c)[1], 0)
        lax.fori_loop(nused_ref[0], ntile, lambda r, c: (spare(r).wait(), c)[1], 0)

    def body(j, carry):
        for k in range(TOP_K):
            dst = dest_ref[k * t_total + i * tm + j]
            pltpu.make_async_copy(h2_ref.at[pl.ds(j, 1), :], xg_hbm.at[pl.ds(dst, 1), :],
                                  sem).start(priority=k % 2)
        return carry

    lax.fori_loop(0, tm, body, 0, unroll=8)
    pltpu.make_async_copy(xg_hbm.at[pl.ds(TOP_K * tm, TOP_K * tm), :],
                          xg_hbm.at[pl.ds(0, TOP_K * tm), :], sem).wait()


def _dispatch(dest, counts, tail_tile, nused, h2, nrow):
    t = h2.shape[0]
    tm = TM_DISPATCH
    grid_spec = pltpu.PrefetchScalarGridSpec(
        num_scalar_prefetch=4,
        grid=(t // tm,),
        in_specs=[pl.BlockSpec((tm, D_MODEL), lambda i, *_: (i, 0))],
        out_specs=pl.BlockSpec(memory_space=pl.ANY),
        scratch_shapes=[pltpu.VMEM((TR, D_MODEL), F32), pltpu.SemaphoreType.DMA(()),
                        pltpu.SemaphoreType.DMA(())],
    )
    return pl.pallas_call(
        _dispatch_kernel,
        grid_spec=grid_spec,
        out_shape=jax.ShapeDtypeStruct((nrow, D_MODEL), F32),
        compiler_params=pltpu.CompilerParams(
            dimension_semantics=("arbitrary",), vmem_limit_bytes=VMEM_LIMIT),
        name="moe_dispatch",
    )(dest, counts, tail_tile, nused, h2)


def _experts_kernel(te_ref, nused_ref, xg_ref, wup_ref, bup_ref, wdn_ref, bdn_ref, y_ref,
                    wup_bf, wdn_bf):
    r = pl.program_id(0)

    @pl.when(r < nused_ref[0])
    def _():
        @pl.when(jnp.logical_or(r == 0, te_ref[r] != te_ref[jnp.maximum(r - 1, 0)]))
        def _():
            wup_bf[...] = wup_ref[0].astype(BF16)
            wdn_bf[...] = wdn_ref[0].astype(BF16)

        x = xg_ref[...].astype(BF16)
        hid = jnp.dot(x, wup_bf[...], preferred_element_type=F32) + bup_ref[0]
        glu = jnp.minimum(hid[:, :D_FF], SWIGLU_LIMIT)
        lin = jnp.clip(hid[:, D_FF:], -SWIGLU_LIMIT, SWIGLU_LIMIT)
        act = glu * _sigmoid(SWIGLU_ALPHA * glu) * (lin + 1.0)
        y_ref[...] = jnp.dot(act.astype(BF16), wdn_bf[...], preferred_element_type=F32) + bdn_ref[0]

    @pl.when(r >= nused_ref[0])
    def _():
        y_ref[...] = jnp.zeros_like(y_ref)


def _experts(tile_expert, nused, xg, w_up, b_up, w_down, b_down):
    ntile = tile_expert.shape[0]
    by_expert = lambda r, te, nu: (te[r], 0, 0)
    by_block = lambda r, te, nu: (r, 0)
    grid_spec = pltpu.PrefetchScalarGridSpec(
        num_scalar_prefetch=2,
        grid=(ntile,),
        in_specs=[
            pl.BlockSpec((TR, D_MODEL), by_block),
            pl.BlockSpec((1, D_MODEL, 2 * D_FF), by_expert),
            pl.BlockSpec((1, 1, 2 * D_FF), by_expert),
            pl.BlockSpec((1, D_FF, D_MODEL), by_expert),
            pl.BlockSpec((1, 1, D_MODEL), by_expert),
        ],
        out_specs=pl.BlockSpec((TR, D_MODEL), by_block),
        scratch_shapes=[pltpu.VMEM((D_MODEL, 2 * D_FF), BF16), pltpu.VMEM((D_FF, D_MODEL), BF16)],
    )
    return pl.pallas_call(
        _experts_kernel,
        grid_spec=grid_spec,
        out_shape=jax.ShapeDtypeStruct(xg.shape, F32),
        compiler_params=pltpu.CompilerParams(
            dimension_semantics=("arbitrary",), vmem_limit_bytes=VMEM_LIMIT),
        name="moe_experts",
    )(tile_expert, nused, xg, w_up, b_up, w_down, b_down)


def _gather_combine_kernel(dest_ref, x1_ref, gk_ref, gfin_ref, y_hbm, o_ref, ybuf, sem):
    i = pl.program_id(0)
    n = pl.num_programs(0)
    tm, t_total = x1_ref.shape[0], dest_ref.shape[0] // TOP_K

    def gather_start(tile, buf):
        def body(j, carry):
            for k in range(TOP_K):
                src = dest_ref[k * t_total + tile * tm + j]
                pltpu.make_async_copy(y_hbm.at[pl.ds(src, 1), :], ybuf.at[buf, k, pl.ds(j, 1), :],
                                      sem.at[buf]).start(priority=k % 2)
            return carry
        lax.fori_loop(0, tm, body, 0, unroll=8)

    @pl.when(i == 0)
    def _():
        gather_start(0, 0)

    @pl.when(i + 1 < n)
    def _():
        gather_start(i + 1, lax.rem(i + 1, 2))

    buf = lax.rem(i, 2)
    pltpu.make_async_copy(ybuf.at[1 - buf], ybuf.at[buf], sem.at[buf]).wait()
    acc = x1_ref[...]
    for k in range(TOP_K):
        acc = acc + gk_ref[k] * ybuf[buf, k]
    o_ref[...] = _rmsnorm(acc, gfin_ref[...])


def _gather_combine(dest, x1, gk_col, g_fin, y):
    t = x1.shape[0]
    tm = TM_COMBINE
    grid_spec = pltpu.PrefetchScalarGridSpec(
        num_scalar_prefetch=1,
        grid=(t // tm,),
        in_specs=[
            pl.BlockSpec((tm, D_MODEL), lambda i, d: (i, 0)),
            pl.BlockSpec((TOP_K, tm, 1), lambda i, d: (0, i, 0)),
            pl.BlockSpec((1, D_MODEL), lambda i, d: (0, 0)),
            pl.BlockSpec(memory_space=pl.ANY),
        ],
        out_specs=pl.BlockSpec((tm, D_MODEL), lambda i, d: (i, 0)),
        scratch_shapes=[pltpu.VMEM((2, TOP_K, tm, D_MODEL), F32), pltpu.SemaphoreType.DMA((2,))],
    )
    return pl.pallas_call(
        _gather_combine_kernel,
        grid_spec=grid_spec,
        out_shape=jax.ShapeDtypeStruct((t, D_MODEL), F32),
        compiler_params=pltpu.CompilerParams(
            dimension_semantics=("arbitrary",), vmem_limit_bytes=VMEM_LIMIT),
        name="gather_combine_norm",
    )(dest, x1, gk_col, g_fin, y)


def kernel(x, norm_mix, w_in, s5_lam_re, s5_lam_im, s5_b_re, s5_b_im, s5_c_re, s5_c_im, s5_d,
           s5_log_step, w_glu, w_branch_sb, w_branch_s5, w_out, norm_ffn, w_router, b_router,
           w_up, b_up, w_down, b_down, norm_final):
    bsz, seq, d = x.shape
    t = bsz * seq
    assert norm_mix.shape[0] == 1, "single-layer block"
    l = 0
    x2 = x.reshape(t, d).astype(F32)
    qkv, u, gates = _inproj(x2, norm_mix[l].reshape(1, d).astype(F32), w_in[l].astype(BF16))
    ysb = _attention(qkv.reshape(bsz, seq, 3 * SB_WIDTH))
    bblk, cblk, tab = _s5_params(s5_lam_re[l], s5_lam_im[l], s5_b_re[l], s5_b_im[l],
                                 s5_c_re[l], s5_c_im[l], s5_log_step[l])
    ys5 = _s5(u.reshape(bsz, seq, S5_WIDTH), bblk, cblk, tab,
              s5_d[l].reshape(1, S5_WIDTH).astype(F32), w_glu[l].astype(BF16))
    x1, h2, ek, pk, gk, cnt = _merge(
        ysb.reshape(t, SB_WIDTH), ys5.reshape(t, S5_WIDTH), gates, x2,
        w_branch_sb[l].astype(BF16), w_branch_s5[l].astype(BF16), w_out[l].astype(BF16),
        norm_ffn[l].reshape(1, d).astype(F32), w_router[l].T.astype(F32),
        b_router[l].reshape(N_EXPERTS, 1).astype(F32))
    dest, counts, tail_tile, tile_expert, nused = _layout(ek, pk, cnt, t)
    xg = _dispatch(dest, counts, tail_tile, nused, h2, tile_expert.shape[0] * TR)
    y = _experts(tile_expert, nused, xg, w_up[l],
                 b_up[l].reshape(N_EXPERTS, 1, 2 * D_FF), w_down[l],
                 b_down[l].reshape(N_EXPERTS, 1, D_MODEL))
    out = _gather_combine(dest, x1, gk.reshape(TOP_K, t, 1),
                          norm_final.reshape(1, d).astype(F32), y)
    return out.reshape(bsz, seq, d).astype(x.dtype)
```

```python
import jax
import jax.numpy as jnp
from jax import lax
from jax.experimental import pallas as pl
from jax.experimental.pallas import tpu as pltpu

F32 = jnp.float32
BF16 = jnp.bfloat16

D_MODEL = 1024
SB_HEADS = 8
SB_HEAD_DIM = 64
SB_WIDTH = SB_HEADS * SB_HEAD_DIM
S5_WIDTH = 512
S5_GROUP = 16
S5_GROUPS = 32
S5_STATE = 64
N_EXPERTS = 32
TOP_K = 4
D_FF = 1024
SWIGLU_LIMIT = 7.0
SWIGLU_ALPHA = 1.702
RMS_EPS = 1e-5
OFF_U = 3 * SB_WIDTH
OFF_GATES = OFF_U + S5_WIDTH
IN_WIDTH = OFF_GATES + 2 * D_MODEL

LANES = 128
SUBLANES = 8
VMEM_LIMIT = 56 * 1024 * 1024

TM_PROJ = 512
TQ = 256
TK = 256
EXP_UNDERFLOW = 110.0
TC_S5 = 256
S5_BLOCKS = 4
TM_MERGE = 512
SUB_MERGE = 256
TR = 512
TM_DISPATCH = 256
STAGE_ROWS = TOP_K * TM_DISPATCH + N_EXPERTS * SUBLANES


def _rmsnorm(x, g):
    return x * lax.rsqrt(jnp.mean(x * x, axis=-1, keepdims=True) + RMS_EPS) * g


def _sigmoid(x):
    return 1.0 / (1.0 + jnp.exp(-x))


def _inproj_kernel(x_ref, g_ref, w_ref, qkv_ref, u_ref, gate_ref):
    hb = _rmsnorm(x_ref[...], g_ref[...]).astype(BF16)
    chunk = SB_WIDTH

    def proj(c0):
        return jnp.dot(hb, w_ref[:, c0:c0 + chunk], preferred_element_type=F32)

    qkv_ref[:, 0:chunk] = (proj(0) * (SB_HEAD_DIM ** -0.5)).astype(BF16)
    qkv_ref[:, chunk:2 * chunk] = proj(chunk).astype(BF16)
    qkv_ref[:, 2 * chunk:3 * chunk] = proj(2 * chunk).astype(BF16)
    u_ref[...] = proj(OFF_U).astype(BF16)
    for c in range(2 * D_MODEL // chunk):
        gate_ref[:, c * chunk:(c + 1) * chunk] = proj(OFF_GATES + c * chunk).astype(BF16)


def _inproj(x2, g, w_bf):
    t = x2.shape[0]
    return pl.pallas_call(
        _inproj_kernel,
        grid=(t // TM_PROJ,),
        in_specs=[
            pl.BlockSpec((TM_PROJ, D_MODEL), lambda i: (i, 0)),
            pl.BlockSpec((1, D_MODEL), lambda i: (0, 0)),
            pl.BlockSpec((D_MODEL, IN_WIDTH), lambda i: (0, 0)),
        ],
        out_specs=[
            pl.BlockSpec((TM_PROJ, 3 * SB_WIDTH), lambda i: (i, 0)),
            pl.BlockSpec((TM_PROJ, S5_WIDTH), lambda i: (i, 0)),
            pl.BlockSpec((TM_PROJ, 2 * D_MODEL), lambda i: (i, 0)),
        ],
        out_shape=[
            jax.ShapeDtypeStruct((t, 3 * SB_WIDTH), BF16),
            jax.ShapeDtypeStruct((t, S5_WIDTH), BF16),
            jax.ShapeDtypeStruct((t, 2 * D_MODEL), BF16),
        ],
        compiler_params=pltpu.CompilerParams(
            dimension_semantics=("parallel",), vmem_limit_bytes=VMEM_LIMIT),
        name="inproj",
    )(x2, g, w_bf)


def _attn_kernel(q_ref, k_ref, v_ref, o_ref, acc_ref, cb0_ref, cb1_ref):
    qi = pl.program_id(2)
    q = q_ref[0]
    is_h0 = lax.broadcasted_iota(jnp.int32, (TQ, LANES), 1) < SB_HEAD_DIM
    row = lax.broadcasted_iota(jnp.int32, (TQ, TK), 0)
    col = lax.broadcasted_iota(jnp.int32, (TQ, TK), 1)
    tri = jnp.where(row >= col, 1.0, 0.0).astype(BF16)
    causal = col < row
    zero = jnp.zeros_like(q)
    qms = (jnp.where(is_h0, q, zero), jnp.where(is_h0, zero, q))
    cbs = (cb0_ref, cb1_ref)
    acc_ref[...] = jnp.zeros_like(acc_ref)
    cb0_ref[...] = jnp.zeros_like(cb0_ref)
    cb1_ref[...] = jnp.zeros_like(cb1_ref)

    def scores(qm, k, mask):
        z = lax.dot_general(qm, k, (((1,), (1,)), ((), ())), preferred_element_type=F32)
        log_keep = -(jnp.maximum(z, 0.0) + jnp.log(1.0 + jnp.exp(-jnp.abs(z))))
        if mask is not None:
            log_keep = jnp.where(mask, log_keep, 0.0)
        hi = log_keep.astype(BF16)
        lo = (log_keep - hi.astype(F32)).astype(BF16)
        rcum = (jnp.dot(hi, tri, preferred_element_type=F32)
                + jnp.dot(lo, tri, preferred_element_type=F32))
        return z, rcum

    def pair(kt, diag):
        has_b = kt >= 1
        ka = pl.multiple_of(kt * TK, TK)
        kb = pl.multiple_of(jnp.maximum(kt - 1, 0) * TK, TK)
        k_a, v_a = k_ref[0, pl.ds(ka, TK), :], v_ref[0, pl.ds(ka, TK), :]
        k_b, v_b = k_ref[0, pl.ds(kb, TK), :], v_ref[0, pl.ds(kb, TK), :]
        mask_a = causal if diag else None
        pvs = []
        for qm, cb_ref in zip(qms, cbs):
            z_a, rcum_a = scores(qm, k_a, mask_a)
            z_b, rcum_b = scores(qm, k_b, has_b)
            cb = cb_ref[...]
            cb_a = cb + rcum_a[:, 0:1]
            w_a = jnp.exp(z_a + rcum_a + cb)
            if diag:
                w_a = jnp.where(causal, w_a, 0.0)
            w_b = jnp.where(has_b, jnp.exp(z_b + rcum_b + cb_a), 0.0)
            pvs.append(jnp.dot(w_a.astype(BF16), v_a, preferred_element_type=F32)
                       + jnp.dot(w_b.astype(BF16), v_b, preferred_element_type=F32))
            cb_ref[...] = cb_a + rcum_b[:, 0:1]
        acc_ref[...] += jnp.where(is_h0, pvs[0], pvs[1])
        worst = jnp.maximum(jnp.max(cb0_ref[...]), jnp.max(cb1_ref[...]))
        return jnp.logical_not(worst < -EXP_UNDERFLOW)

    live = pair(qi, True)

    def cond(carry):
        kt, live = carry
        return jnp.logical_and(kt >= 0, live)

    def body(carry):
        kt, _ = carry
        return kt - 2, pair(kt, False)

    lax.while_loop(cond, body, (qi - 2, live))
    o_ref[0] = acc_ref[...].astype(BF16)


def _attention(qkv3):
    b, l, _ = qkv3.shape
    pairs = SB_WIDTH // LANES
    return pl.pallas_call(
        _attn_kernel,
        grid=(b, pairs, l // TQ),
        in_specs=[
            pl.BlockSpec((1, TQ, LANES), lambda bi, hp, qi: (bi, qi, hp)),
            pl.BlockSpec((1, l, LANES), lambda bi, hp, qi: (bi, 0, pairs + hp)),
            pl.BlockSpec((1, l, LANES), lambda bi, hp, qi: (bi, 0, 2 * pairs + hp)),
        ],
        out_specs=pl.BlockSpec((1, TQ, LANES), lambda bi, hp, qi: (bi, qi, hp)),
        out_shape=jax.ShapeDtypeStruct((b, l, SB_WIDTH), BF16),
        scratch_shapes=[pltpu.VMEM((TQ, LANES), F32), pltpu.VMEM((TQ, 1), F32),
                        pltpu.VMEM((TQ, 1), F32)],
        compiler_params=pltpu.CompilerParams(
            dimension_semantics=("parallel", "parallel", "arbitrary"),
            vmem_limit_bytes=VMEM_LIMIT),
        name="sb_attention",
    )(qkv3, qkv3, qkv3)


def _s5_kernel(u_ref, bblk_ref, cblk_ref, tab_ref, d_ref, wglu_ref, o_ref,
               sre_ref, sim_ref, state_ref, y_ref):
    nb = u_ref.shape[0]
    half = S5_GROUPS * S5_STATE // S5_BLOCKS
    cin = S5_WIDTH // S5_BLOCKS
    ntiles = S5_GROUPS * S5_STATE // LANES

    @pl.when(pl.program_id(0) == 0)
    def _():
        state_ref[...] = jnp.zeros_like(state_ref)

    for b in range(nb):
        for blk in range(S5_BLOCKS):
            bu = jnp.dot(u_ref[b, :, blk * cin:(blk + 1) * cin], bblk_ref[blk],
                         preferred_element_type=F32)
            sre_ref[b, :, blk * half:(blk + 1) * half] = bu[:, :half]
            sim_ref[b, :, blk * half:(blk + 1) * half] = bu[:, half:]

    for j in range(ntiles):
        lanes = slice(j * LANES, (j + 1) * LANES)
        tabs = [tab_ref[i, :, lanes] for i in range(8)]

        def group(g, carry):
            rows = pl.ds(pl.multiple_of(g * SUBLANES, SUBLANES), SUBLANES)
            out = []
            for b in range(nb):
                xr, xi = sre_ref[b, rows, lanes], sim_ref[b, rows, lanes]
                for stage in range(3):
                    ar, ai = tabs[2 * stage], tabs[2 * stage + 1]
                    sr = pltpu.roll(xr, 1 << stage, axis=0)
                    si = pltpu.roll(xi, 1 << stage, axis=0)
                    xr, xi = xr + (ar * sr - ai * si), xi + (ar * si + ai * sr)
                cr, ci = carry[2 * b], carry[2 * b + 1]
                xr, xi = xr + (tabs[6] * cr - tabs[7] * ci), xi + (tabs[6] * ci + tabs[7] * cr)
                sre_ref[b, rows, lanes] = xr
                sim_ref[b, rows, lanes] = xi
                out.append(jnp.broadcast_to(xr[SUBLANES - 1:, :], xr.shape))
                out.append(jnp.broadcast_to(xi[SUBLANES - 1:, :], xi.shape))
            return tuple(out)

        init = tuple(state_ref[c, :, lanes] for c in range(2 * nb))
        final = lax.fori_loop(0, TC_S5 // SUBLANES, group, init)
        for c in range(2 * nb):
            state_ref[c, :, lanes] = final[c]

    for b in range(nb):
        for blk in range(S5_BLOCKS):
            sre = sre_ref[b, :, blk * half:(blk + 1) * half].astype(BF16)
            sim = sim_ref[b, :, blk * half:(blk + 1) * half].astype(BF16)
            y_ref[:, blk * cin:(blk + 1) * cin] = (
                jnp.dot(sre, cblk_ref[blk, :half, :], preferred_element_type=F32)
                + jnp.dot(sim, cblk_ref[blk, half:, :], preferred_element_type=F32))
        y = y_ref[...] + d_ref[...] * u_ref[b].astype(F32)
        y = jax.nn.gelu(y)
        glu = jnp.dot(y.astype(BF16), wglu_ref[...], preferred_element_type=F32)
        o_ref[b] = (y * _sigmoid(glu)).astype(BF16)


def _s5_params(lam_re, lam_im, b_re, b_im, c_re, c_im, log_step):
    g, p, h = S5_GROUPS, S5_STATE, S5_GROUP
    lr, li = lam_re.astype(F32), lam_im.astype(F32)
    step = jnp.exp(log_step.astype(F32))[:, None]
    mag = jnp.exp(lr * step)
    bar_re, bar_im = mag * jnp.cos(li * step), mag * jnp.sin(li * step)
    den = lr * lr + li * li
    f_re = ((bar_re - 1.0) * lr + bar_im * li) / den
    f_im = (bar_im * lr - (bar_re - 1.0) * li) / den
    f_re, f_im = f_re[:, :, None], f_im[:, :, None]
    bb_re = f_re * b_re.astype(F32) - f_im * b_im.astype(F32)
    bb_im = f_re * b_im.astype(F32) + f_im * b_re.astype(F32)
    gpb = g // S5_BLOCKS
    eye = jnp.eye(gpb, dtype=F32)

    def in_block(m):
        m = m.reshape(S5_BLOCKS, gpb, p, h)
        return jnp.einsum('kgph,gf->kghfp', m, eye).reshape(S5_BLOCKS, gpb * h, gpb * p)

    def out_block(m):
        m = m.reshape(S5_BLOCKS, gpb, h, p)
        return jnp.einsum('kghp,gf->kgpfh', m, eye).reshape(S5_BLOCKS, gpb * p, gpb * h)

    bblk = jnp.concatenate([in_block(bb_re), in_block(bb_im)], axis=2)
    cblk = jnp.concatenate([out_block(c_re.astype(F32)), out_block(-c_im.astype(F32))], axis=1)

    l_re, l_im = bar_re.reshape(1, g * p), bar_im.reshape(1, g * p)
    pows = [(l_re, l_im)]
    for _ in range(SUBLANES - 1):
        q_re, q_im = pows[-1]
        pows.append((q_re * l_re - q_im * l_im, q_re * l_im + q_im * l_re))
    sub = jnp.arange(SUBLANES)[:, None]
    rows = []
    for shift in (1, 2, 4):
        for part in pows[shift - 1]:
            rows.append(jnp.where(sub >= shift, part, 0.0))
    rows.append(jnp.concatenate([q[0] for q in pows], axis=0))
    rows.append(jnp.concatenate([q[1] for q in pows], axis=0))
    return bblk.astype(BF16), cblk.astype(BF16), jnp.stack(rows)


def _s5(u3, bblk, cblk, tab, d_row, wglu_bf):
    b, l, _ = u3.shape
    nstate = S5_GROUPS * S5_STATE
    const = lambda *shape: pl.BlockSpec(shape, lambda i: (0,) * len(shape))
    return pl.pallas_call(
        _s5_kernel,
        grid=(l // TC_S5,),
        in_specs=[
            pl.BlockSpec((b, TC_S5, S5_WIDTH), lambda i: (0, i, 0)),
            const(*bblk.shape), const(*cblk.shape), const(*tab.shape),
            const(1, S5_WIDTH), const(S5_WIDTH, S5_WIDTH),
        ],
        out_specs=pl.BlockSpec((b, TC_S5, S5_WIDTH), lambda i: (0, i, 0)),
        out_shape=jax.ShapeDtypeStruct((b, l, S5_WIDTH), BF16),
        scratch_shapes=[
            pltpu.VMEM((b, TC_S5, nstate), F32),
            pltpu.VMEM((b, TC_S5, nstate), F32),
            pltpu.VMEM((2 * b, SUBLANES, nstate), F32),
            pltpu.VMEM((TC_S5, S5_WIDTH), F32),
        ],
        compiler_params=pltpu.CompilerParams(
            dimension_semantics=("arbitrary",), vmem_limit_bytes=VMEM_LIMIT),
        name="s5_scan",
    )(u3, bblk, cblk, tab, d_row, wglu_bf)


def _merge_kernel(ysb_ref, ys5_ref, gate_ref, x_ref, wsb_ref, ws5_ref, wout_ref, g_ref,
                  wr_ref, br_ref, x1_ref, h2_ref, ek_ref, pk_ref, gk_ref, cnt_ref, carry_ref):
    @pl.when(pl.program_id(0) == 0)
    def _():
        carry_ref[...] = jnp.zeros_like(carry_ref)

    sub = SUB_MERGE
    trow = lax.broadcasted_iota(jnp.int32, (sub, sub), 0)
    tcol = lax.broadcasted_iota(jnp.int32, (sub, sub), 1)
    before = jnp.where(trow < tcol, 1.0, 0.0).astype(BF16)
    erow = lax.broadcasted_iota(jnp.int32, (N_EXPERTS, N_EXPERTS), 0)
    ecol = lax.broadcasted_iota(jnp.int32, (N_EXPERTS, N_EXPERTS), 1)
    lower = jnp.where(ecol < erow, 1.0, 0.0).astype(BF16)
    eidx = lax.broadcasted_iota(jnp.int32, (N_EXPERTS, sub), 0)
    eidx_f = eidx.astype(F32)

    for s in range(x_ref.shape[0] // sub):
        rows = slice(s * sub, (s + 1) * sub)
        a = jnp.dot(ysb_ref[rows, :], wsb_ref[...], preferred_element_type=F32)
        b = jnp.dot(ys5_ref[rows, :], ws5_ref[...], preferred_element_type=F32)
        ga = _sigmoid(gate_ref[rows, :D_MODEL].astype(F32))
        gb = _sigmoid(gate_ref[rows, D_MODEL:].astype(F32))
        merged = (ga * a + gb * b).astype(BF16)
        x1 = x_ref[rows, :] + jnp.dot(merged, wout_ref[...], preferred_element_type=F32)
        x1_ref[rows, :] = x1
        h2 = _rmsnorm(x1, g_ref[...])
        h2_ref[rows, :] = h2

        logits = lax.dot_general(wr_ref[...], h2, (((1,), (1,)), ((), ())),
                                 precision=lax.Precision.HIGHEST,
                                 preferred_element_type=F32) + br_ref[...]
        work = logits
        sel = jnp.zeros(logits.shape, jnp.bool_)
        top = None
        for k in range(TOP_K):
            m = jnp.max(work, axis=0, keepdims=True)
            first = jnp.min(jnp.where(work == m, eidx, N_EXPERTS), axis=0, keepdims=True)
            pick = eidx == first
            sel = jnp.logical_or(sel, pick)
            work = jnp.where(pick, -jnp.inf, work)
            if k == 0:
                top = m
        e = jnp.where(sel, jnp.exp(logits - top), 0.0)
        gate = e / jnp.sum(e, axis=0, keepdims=True)

        mask = jnp.where(sel, 1.0, 0.0)
        pos = jnp.dot(mask.astype(BF16), before, preferred_element_type=F32) + carry_ref[...]
        carry_ref[...] += jnp.sum(mask, axis=1, keepdims=True)

        rank = jnp.dot(lower, mask.astype(BF16), preferred_element_type=F32)
        for k in range(TOP_K):
            ind = jnp.logical_and(sel, rank == float(k))
            pick1 = lambda v: jnp.sum(jnp.where(ind, v, 0.0), axis=0, keepdims=True)
            ek_ref[k:k + 1, rows] = pick1(eidx_f).astype(jnp.int32)
            pk_ref[k:k + 1, rows] = pick1(pos).astype(jnp.int32)
            gk_ref[k:k + 1, rows] = pick1(gate)

    cnt_ref[...] = jnp.broadcast_to(carry_ref[...], cnt_ref.shape)


def _merge(ysb, ys5, gates, x2, wsb, ws5, wout, g_ffn, wr_t, br_col):
    t = x2.shape[0]
    tm = TM_MERGE
    const = lambda *shape: pl.BlockSpec(shape, lambda i: (0,) * len(shape))
    return pl.pallas_call(
        _merge_kernel,
        grid=(t // tm,),
        in_specs=[
            pl.BlockSpec((tm, SB_WIDTH), lambda i: (i, 0)),
            pl.BlockSpec((tm, S5_WIDTH), lambda i: (i, 0)),
            pl.BlockSpec((tm, 2 * D_MODEL), lambda i: (i, 0)),
            pl.BlockSpec((tm, D_MODEL), lambda i: (i, 0)),
            const(SB_WIDTH, D_MODEL), const(S5_WIDTH, D_MODEL), const(D_MODEL, D_MODEL),
            const(1, D_MODEL), const(N_EXPERTS, D_MODEL), const(N_EXPERTS, 1),
        ],
        out_specs=[
            pl.BlockSpec((tm, D_MODEL), lambda i: (i, 0)),
            pl.BlockSpec((tm, D_MODEL), lambda i: (i, 0)),
            pl.BlockSpec((TOP_K, tm), lambda i: (0, i)),
            pl.BlockSpec((TOP_K, tm), lambda i: (0, i)),
            pl.BlockSpec((TOP_K, tm), lambda i: (0, i)),
            pl.BlockSpec((N_EXPERTS, LANES), lambda i: (0, 0)),
        ],
        out_shape=[
            jax.ShapeDtypeStruct((t, D_MODEL), F32),
            jax.ShapeDtypeStruct((t, D_MODEL), F32),
            jax.ShapeDtypeStruct((TOP_K, t), jnp.int32),
            jax.ShapeDtypeStruct((TOP_K, t), jnp.int32),
            jax.ShapeDtypeStruct((TOP_K, t), F32),
            jax.ShapeDtypeStruct((N_EXPERTS, LANES), F32),
        ],
        scratch_shapes=[pltpu.VMEM((N_EXPERTS, 1), F32)],
        compiler_params=pltpu.CompilerParams(
            dimension_semantics=("arbitrary",), vmem_limit_bytes=VMEM_LIMIT),
        name="merge_router",
    )(ysb, ys5, gates, x2, wsb, ws5, wout, g_ffn, wr_t, br_col)


def _layout(ek, pk, t):
    tm = TM_DISPATCH
    ni = t // tm
    ek3, pk3 = ek.reshape(TOP_K, ni, tm), pk.reshape(TOP_K, ni, tm)
    cnt = jnp.stack([jnp.sum(ek3 == e, axis=(0, 2)) for e in range(N_EXPERTS)], axis=1)
    cnt = cnt.astype(jnp.int32)
    seg = (cnt + SUBLANES - 1) // SUBLANES * SUBLANES
    group = jnp.sum(seg, axis=0)
    padded = (group + TR - 1) // TR * TR
    ends = jnp.cumsum(padded)
    dst = (ends - padded)[None, :] + jnp.cumsum(seg, axis=0) - seg
    stage = jnp.cumsum(seg, axis=1) - seg
    base = stage - (jnp.cumsum(cnt, axis=0) - cnt)
    slot = pk3
    for e in range(N_EXPERTS):
        slot = slot + jnp.where(ek3 == e, base[None, :, e, None], 0)
    ntile = (TOP_K * t + ni * N_EXPERTS * (SUBLANES - 1)) // TR + 1 + N_EXPERTS
    nused = ends[-1] // TR
    tile_start = jnp.arange(ntile, dtype=jnp.int32) * TR
    tile_expert = jnp.sum(tile_start[:, None] >= ends[None, :], axis=1).astype(jnp.int32)
    tile_expert = jnp.minimum(tile_expert, jnp.take(tile_expert, nused - 1))
    tail_tile = jnp.maximum(ends // TR - 1, 0).astype(jnp.int32)
    segs = (stage.reshape(-1), dst.reshape(-1).astype(jnp.int32), seg.reshape(-1))
    return (slot.reshape(TOP_K, t), segs, group.astype(jnp.int32), tail_tile, tile_expert,
            nused.reshape(1).astype(jnp.int32), ntile)


def _segment_copy(stage_ref, dst_ref, len_ref, idx, staging, buffer, sem, to_buffer):
    aligned = lambda v: pl.multiple_of(v, SUBLANES)
    n = aligned(len_ref[idx])
    a = staging.at[pl.ds(aligned(stage_ref[idx]), n), :]
    b = buffer.at[pl.ds(aligned(dst_ref[idx]), n), :]
    return pltpu.make_async_copy(a, b, sem) if to_buffer else pltpu.make_async_copy(b, a, sem)


def _dispatch_kernel(stage_ref, dst_ref, len_ref, grp_ref, tail_ref, nused_ref, slot_ref, h2_ref,
                     xg_hbm, s_ref, zero_ref, sem, zsem):
    i = pl.program_id(0)

    @pl.when(i == 0)
    def _():
        zero_ref[...] = jnp.zeros_like(zero_ref)
        fill = lambda e: pltpu.make_async_copy(
            zero_ref, xg_hbm.at[pl.ds(pl.multiple_of(tail_ref[e] * TR, TR), TR), :], zsem)
        for e in range(N_EXPERTS):
            @pl.when(grp_ref[e] > 0)
            def _():
                fill(e).start()
        for e in range(N_EXPERTS):
            @pl.when(grp_ref[e] > 0)
            def _():
                fill(e).wait()

        spare = lambda r: pltpu.make_async_copy(
            zero_ref, xg_hbm.at[pl.ds(pl.multiple_of(r * TR, TR), TR), :], zsem)
        ntile = xg_hbm.shape[0] // TR
        lax.fori_loop(nused_ref[0], ntile, lambda r, c: (spare(r).start(), c)[1], 0)
        lax.fori_loop(nused_ref[0], ntile, lambda r, c: (spare(r).wait(), c)[1], 0)

    rowid = lax.broadcasted_iota(jnp.int32, (s_ref.shape[0], h2_ref.shape[0]), 0)
    hit = slot_ref[0:1, :] == rowid
    for k in range(1, TOP_K):
        hit = jnp.logical_or(hit, slot_ref[k:k + 1, :] == rowid)
    onehot = jnp.where(hit, 1.0, 0.0).astype(BF16)
    s_ref[...] = jnp.dot(onehot, h2_ref[...].astype(BF16), preferred_element_type=F32)

    def each_segment(action):
        def body(e, carry):
            idx = i * N_EXPERTS + e

            @pl.when(len_ref[idx] > 0)
            def _():
                action(_segment_copy(stage_ref, dst_ref, len_ref, idx, s_ref, xg_hbm, sem, True))
            return carry
        lax.fori_loop(0, N_EXPERTS, body, 0)

    each_segment(lambda copy: copy.start())
    each_segment(lambda copy: copy.wait())


def _dispatch(slot, segs, group, tail_tile, nused, h2, nrow):
    t = h2.shape[0]
    tm = TM_DISPATCH
    grid_spec = pltpu.PrefetchScalarGridSpec(
        num_scalar_prefetch=6,
        grid=(t // tm,),
        in_specs=[pl.BlockSpec((TOP_K, tm), lambda i, *_: (0, i)),
                  pl.BlockSpec((tm, D_MODEL), lambda i, *_: (i, 0))],
        out_specs=pl.BlockSpec(memory_space=pl.ANY),
        scratch_shapes=[pltpu.VMEM((STAGE_ROWS, D_MODEL), F32), pltpu.VMEM((TR, D_MODEL), F32),
                        pltpu.SemaphoreType.DMA(()), pltpu.SemaphoreType.DMA(())],
    )
    return pl.pallas_call(
        _dispatch_kernel,
        grid_spec=grid_spec,
        out_shape=jax.ShapeDtypeStruct((nrow, D_MODEL), F32),
        compiler_params=pltpu.CompilerParams(
            dimension_semantics=("arbitrary",), vmem_limit_bytes=VMEM_LIMIT),
        name="moe_dispatch",
    )(*segs, group, tail_tile, nused, slot, h2)


def _experts_kernel(te_ref, nused_ref, xg_ref, wup_ref, bup_ref, wdn_ref, bdn_ref, y_ref,
                    wup_bf, wdn_bf):
    r = pl.program_id(0)

    @pl.when(r < nused_ref[0])
    def _():
        @pl.when(jnp.logical_or(r == 0, te_ref[r] != te_ref[jnp.maximum(r - 1, 0)]))
        def _():
            wup_bf[...] = wup_ref[0].astype(BF16)
            wdn_bf[...] = wdn_ref[0].astype(BF16)

        x = xg_ref[...].astype(BF16)
        hid = jnp.dot(x, wup_bf[...], preferred_element_type=F32) + bup_ref[0]
        glu = jnp.minimum(hid[:, :D_FF], SWIGLU_LIMIT)
        lin = jnp.clip(hid[:, D_FF:], -SWIGLU_LIMIT, SWIGLU_LIMIT)
        act = glu * _sigmoid(SWIGLU_ALPHA * glu) * (lin + 1.0)
        y_ref[...] = jnp.dot(act.astype(BF16), wdn_bf[...], preferred_element_type=F32) + bdn_ref[0]

    @pl.when(r >= nused_ref[0])
    def _():
        y_ref[...] = jnp.zeros_like(y_ref)


def _experts(tile_expert, nused, xg, w_up, b_up, w_down, b_down):
    ntile = tile_expert.shape[0]
    by_expert = lambda r, te, nu: (te[r], 0, 0)
    by_block = lambda r, te, nu: (r, 0)
    grid_spec = pltpu.PrefetchScalarGridSpec(
        num_scalar_prefetch=2,
        grid=(ntile,),
        in_specs=[
            pl.BlockSpec((TR, D_MODEL), by_block),
            pl.BlockSpec((1, D_MODEL, 2 * D_FF), by_expert),
            pl.BlockSpec((1, 1, 2 * D_FF), by_expert),
            pl.BlockSpec((1, D_FF, D_MODEL), by_expert),
            pl.BlockSpec((1, 1, D_MODEL), by_expert),
        ],
        out_specs=pl.BlockSpec((TR, D_MODEL), by_block),
        scratch_shapes=[pltpu.VMEM((D_MODEL, 2 * D_FF), BF16), pltpu.VMEM((D_FF, D_MODEL), BF16)],
    )
    return pl.pallas_call(
        _experts_kernel,
        grid_spec=grid_spec,
        out_shape=jax.ShapeDtypeStruct(xg.shape, F32),
        compiler_params=pltpu.CompilerParams(
            dimension_semantics=("arbitrary",), vmem_limit_bytes=VMEM_LIMIT),
        name="moe_experts",
    )(tile_expert, nused, xg, w_up, b_up, w_down, b_down)


def _gather_combine_kernel(stage_ref, dst_ref, len_ref, slot_ref, gk_ref, x1_ref, gfin_ref, y_hbm,
                           o_ref, sbuf, sem):
    i = pl.program_id(0)
    nstep = pl.num_programs(0)

    def each_segment(tile, buf, action):
        def body(e, carry):
            idx = tile * N_EXPERTS + e

            @pl.when(len_ref[idx] > 0)
            def _():
                action(_segment_copy(stage_ref, dst_ref, len_ref, idx, sbuf.at[buf], y_hbm,
                                     sem.at[buf], False))
            return carry
        lax.fori_loop(0, N_EXPERTS, body, 0)

    @pl.when(i == 0)
    def _():
        sbuf[...] = jnp.zeros_like(sbuf)
        each_segment(0, 0, lambda copy: copy.start())

    @pl.when(i + 1 < nstep)
    def _():
        each_segment(i + 1, lax.rem(i + 1, 2), lambda copy: copy.start())

    buf = lax.rem(i, 2)
    each_segment(i, buf, lambda copy: copy.wait())

    lane = lax.broadcasted_iota(jnp.int32, (x1_ref.shape[0], sbuf.shape[1]), 1)
    weight = jnp.where(slot_ref[0] == lane, gk_ref[0], 0.0)
    for k in range(1, TOP_K):
        weight = weight + jnp.where(slot_ref[k] == lane, gk_ref[k], 0.0)
    hi = weight.astype(BF16)
    lo = (weight - hi.astype(F32)).astype(BF16)
    rows = sbuf[buf].astype(BF16)
    acc = (x1_ref[...] + jnp.dot(hi, rows, preferred_element_type=F32)
           + jnp.dot(lo, rows, preferred_element_type=F32))
    o_ref[...] = _rmsnorm(acc, gfin_ref[...])


def _gather_combine(slot_col, segs, x1, gk_col, g_fin, y):
    t = x1.shape[0]
    tm = TM_DISPATCH
    grid_spec = pltpu.PrefetchScalarGridSpec(
        num_scalar_prefetch=3,
        grid=(t // tm,),
        in_specs=[
            pl.BlockSpec((TOP_K, tm, 1), lambda i, *_: (0, i, 0)),
            pl.BlockSpec((TOP_K, tm, 1), lambda i, *_: (0, i, 0)),
            pl.BlockSpec((tm, D_MODEL), lambda i, *_: (i, 0)),
            pl.BlockSpec((1, D_MODEL), lambda i, *_: (0, 0)),
            pl.BlockSpec(memory_space=pl.ANY),
        ],
        out_specs=pl.BlockSpec((tm, D_MODEL), lambda i, *_: (i, 0)),
        scratch_shapes=[pltpu.VMEM((2, STAGE_ROWS, D_MODEL), F32), pltpu.SemaphoreType.DMA((2,))],
    )
    return pl.pallas_call(
        _gather_combine_kernel,
        grid_spec=grid_spec,
        out_shape=jax.ShapeDtypeStruct((t, D_MODEL), F32),
        compiler_params=pltpu.CompilerParams(
            dimension_semantics=("arbitrary",), vmem_limit_bytes=VMEM_LIMIT),
        name="gather_combine_norm",
    )(*segs, slot_col, gk_col, x1, g_fin, y)


def kernel(x, norm_mix, w_in, s5_lam_re, s5_lam_im, s5_b_re, s5_b_im, s5_c_re, s5_c_im, s5_d,
           s5_log_step, w_glu, w_branch_sb, w_branch_s5, w_out, norm_ffn, w_router, b_router,
           w_up, b_up, w_down, b_down, norm_final):
    bsz, seq, d = x.shape
    t = bsz * seq
    assert norm_mix.shape[0] == 1, "single-layer block"
    l = 0
    x2 = x.reshape(t, d).astype(F32)
    qkv, u, gates = _inproj(x2, norm_mix[l].reshape(1, d).astype(F32), w_in[l].astype(BF16))
    ysb = _attention(qkv.reshape(bsz, seq, 3 * SB_WIDTH))
    bblk, cblk, tab = _s5_params(s5_lam_re[l], s5_lam_im[l], s5_b_re[l], s5_b_im[l],
                                 s5_c_re[l], s5_c_im[l], s5_log_step[l])
    ys5 = _s5(u.reshape(bsz, seq, S5_WIDTH), bblk, cblk, tab,
              s5_d[l].reshape(1, S5_WIDTH).astype(F32), w_glu[l].astype(BF16))
    x1, h2, ek, pk, gk, cnt = _merge(
        ysb.reshape(t, SB_WIDTH), ys5.reshape(t, S5_WIDTH), gates, x2,
        w_branch_sb[l].astype(BF16), w_branch_s5[l].astype(BF16), w_out[l].astype(BF16),
        norm_ffn[l].reshape(1, d).astype(F32), w_router[l].T.astype(F32),
        b_router[l].reshape(N_EXPERTS, 1).astype(F32))
    slot, segs, group, tail_tile, tile_expert, nused, ntile = _layout(ek, pk, t)
    xg = _dispatch(slot, segs, group, tail_tile, nused, h2, ntile * TR)
    y = _experts(tile_expert, nused, xg, w_up[l],
                 b_up[l].reshape(N_EXPERTS, 1, 2 * D_FF), w_down[l],
                 b_down[l].reshape(N_EXPERTS, 1, D_MODEL))
    out = _gather_combine(slot.reshape(TOP_K, t, 1), segs, x1, gk.reshape(TOP_K, t, 1),
                          norm_final.reshape(1, d).astype(F32), y)
    return out.reshape(bsz, seq, d).astype(x.dtype)
```

```python
import jax
import jax.numpy as jnp
from jax import lax
from jax.experimental import pallas as pl
from jax.experimental.pallas import tpu as pltpu

F32 = jnp.float32
BF16 = jnp.bfloat16

D_MODEL = 1024
SB_HEADS = 8
SB_HEAD_DIM = 64
SB_WIDTH = SB_HEADS * SB_HEAD_DIM
S5_WIDTH = 512
S5_GROUP = 16
S5_GROUPS = 32
S5_STATE = 64
N_EXPERTS = 32
TOP_K = 4
D_FF = 1024
SWIGLU_LIMIT = 7.0
SWIGLU_ALPHA = 1.702
RMS_EPS = 1e-5
OFF_U = 3 * SB_WIDTH
OFF_GATES = OFF_U + S5_WIDTH
IN_WIDTH = OFF_GATES + 2 * D_MODEL

LANES = 128
SUBLANES = 8
VMEM_LIMIT = 56 * 1024 * 1024

TM_PROJ = 512
TQ = 256
TK = 256
EXP_UNDERFLOW = 110.0
TC_S5 = 256
S5_BLOCKS = 4
TM_MERGE = 512
SUB_MERGE = 256
TR = 512
TM_DISPATCH = 256
STAGE_ROWS = TOP_K * TM_DISPATCH + N_EXPERTS * SUBLANES


def _rmsnorm(x, g):
    return x * lax.rsqrt(jnp.mean(x * x, axis=-1, keepdims=True) + RMS_EPS) * g


def _sigmoid(x):
    return 1.0 / (1.0 + jnp.exp(-x))


def _inproj_kernel(x_ref, g_ref, w_ref, qkv_ref, u_ref, gate_ref):
    hb = _rmsnorm(x_ref[...], g_ref[...]).astype(BF16)
    chunk = SB_WIDTH

    def proj(c0):
        return jnp.dot(hb, w_ref[:, c0:c0 + chunk], preferred_element_type=F32)

    qkv_ref[:, 0:chunk] = (proj(0) * (SB_HEAD_DIM ** -0.5)).astype(BF16)
    qkv_ref[:, chunk:2 * chunk] = proj(chunk).astype(BF16)
    qkv_ref[:, 2 * chunk:3 * chunk] = proj(2 * chunk).astype(BF16)
    u_ref[...] = proj(OFF_U).astype(BF16)
    for c in range(2 * D_MODEL // chunk):
        gate_ref[:, c * chunk:(c + 1) * chunk] = proj(OFF_GATES + c * chunk).astype(BF16)


def _inproj(x2, g, w_bf):
    t = x2.shape[0]
    return pl.pallas_call(
        _inproj_kernel,
        grid=(t // TM_PROJ,),
        in_specs=[
            pl.BlockSpec((TM_PROJ, D_MODEL), lambda i: (i, 0)),
            pl.BlockSpec((1, D_MODEL), lambda i: (0, 0)),
            pl.BlockSpec((D_MODEL, IN_WIDTH), lambda i: (0, 0)),
        ],
        out_specs=[
            pl.BlockSpec((TM_PROJ, 3 * SB_WIDTH), lambda i: (i, 0)),
            pl.BlockSpec((TM_PROJ, S5_WIDTH), lambda i: (i, 0)),
            pl.BlockSpec((TM_PROJ, 2 * D_MODEL), lambda i: (i, 0)),
        ],
        out_shape=[
            jax.ShapeDtypeStruct((t, 3 * SB_WIDTH), BF16),
            jax.ShapeDtypeStruct((t, S5_WIDTH), BF16),
            jax.ShapeDtypeStruct((t, 2 * D_MODEL), BF16),
        ],
        compiler_params=pltpu.CompilerParams(
            dimension_semantics=("parallel",), vmem_limit_bytes=VMEM_LIMIT),
        name="inproj",
    )(x2, g, w_bf)


def _attn_kernel(q_ref, k_ref, v_ref, o_ref, acc_ref, cb0_ref, cb1_ref):
    qi = pl.program_id(2)
    q = q_ref[0]
    is_h0 = lax.broadcasted_iota(jnp.int32, (TQ, LANES), 1) < SB_HEAD_DIM
    row = lax.broadcasted_iota(jnp.int32, (TQ, TK), 0)
    col = lax.broadcasted_iota(jnp.int32, (TQ, TK), 1)
    tri = jnp.where(row >= col, 1.0, 0.0).astype(BF16)
    causal = col < row
    zero = jnp.zeros_like(q)
    qms = (jnp.where(is_h0, q, zero), jnp.where(is_h0, zero, q))
    cbs = (cb0_ref, cb1_ref)
    acc_ref[...] = jnp.zeros_like(acc_ref)
    cb0_ref[...] = jnp.zeros_like(cb0_ref)
    cb1_ref[...] = jnp.zeros_like(cb1_ref)

    def scores(qm, k, mask):
        z = lax.dot_general(qm, k, (((1,), (1,)), ((), ())), preferred_element_type=F32)
        log_keep = -(jnp.maximum(z, 0.0) + jnp.log(1.0 + jnp.exp(-jnp.abs(z))))
        if mask is not None:
            log_keep = jnp.where(mask, log_keep, 0.0)
        hi = log_keep.astype(BF16)
        lo = (log_keep - hi.astype(F32)).astype(BF16)
        rcum = (jnp.dot(hi, tri, preferred_element_type=F32)
                + jnp.dot(lo, tri, preferred_element_type=F32))
        return z, rcum

    def pair(kt, diag):
        has_b = kt >= 1
        ka = pl.multiple_of(kt * TK, TK)
        kb = pl.multiple_of(jnp.maximum(kt - 1, 0) * TK, TK)
        k_a, v_a = k_ref[0, pl.ds(ka, TK), :], v_ref[0, pl.ds(ka, TK), :]
        k_b, v_b = k_ref[0, pl.ds(kb, TK), :], v_ref[0, pl.ds(kb, TK), :]
        mask_a = causal if diag else None
        pvs = []
        for qm, cb_ref in zip(qms, cbs):
            z_a, rcum_a = scores(qm, k_a, mask_a)
            z_b, rcum_b = scores(qm, k_b, has_b)
            cb = cb_ref[...]
            cb_a = cb + rcum_a[:, 0:1]
            w_a = jnp.exp(z_a + rcum_a + cb)
            if diag:
                w_a = jnp.where(causal, w_a, 0.0)
            w_b = jnp.where(has_b, jnp.exp(z_b + rcum_b + cb_a), 0.0)
            pvs.append(jnp.dot(w_a.astype(BF16), v_a, preferred_element_type=F32)
                       + jnp.dot(w_b.astype(BF16), v_b, preferred_element_type=F32))
            cb_ref[...] = cb_a + rcum_b[:, 0:1]
        acc_ref[...] += jnp.where(is_h0, pvs[0], pvs[1])
        worst = jnp.maximum(jnp.max(cb0_ref[...]), jnp.max(cb1_ref[...]))
        return jnp.logical_not(worst < -EXP_UNDERFLOW)

    live = pair(qi, True)

    def cond(carry):
        kt, live = carry
        return jnp.logical_and(kt >= 0, live)

    def body(carry):
        kt, _ = carry
        return kt - 2, pair(kt, False)

    lax.while_loop(cond, body, (qi - 2, live))
    o_ref[0] = acc_ref[...].astype(BF16)


def _attention(qkv3):
    b, l, _ = qkv3.shape
    pairs = SB_WIDTH // LANES
    return pl.pallas_call(
        _attn_kernel,
        grid=(b, pairs, l // TQ),
        in_specs=[
            pl.BlockSpec((1, TQ, LANES), lambda bi, hp, qi: (bi, qi, hp)),
            pl.BlockSpec((1, l, LANES), lambda bi, hp, qi: (bi, 0, pairs + hp)),
            pl.BlockSpec((1, l, LANES), lambda bi, hp, qi: (bi, 0, 2 * pairs + hp)),
        ],
        out_specs=pl.BlockSpec((1, TQ, LANES), lambda bi, hp, qi: (bi, qi, hp)),
        out_shape=jax.ShapeDtypeStruct((b, l, SB_WIDTH), BF16),
        scratch_shapes=[pltpu.VMEM((TQ, LANES), F32), pltpu.VMEM((TQ, 1), F32),
                        pltpu.VMEM((TQ, 1), F32)],
        compiler_params=pltpu.CompilerParams(
            dimension_semantics=("parallel", "parallel", "arbitrary"),
            vmem_limit_bytes=VMEM_LIMIT),
        name="sb_attention",
    )(qkv3, qkv3, qkv3)


def _s5_kernel(u_ref, bblk_ref, cblk_ref, tab_ref, d_ref, wglu_ref, o_ref,
               sre_ref, sim_ref, state_ref, y_ref):
    nb = u_ref.shape[0]
    half = S5_GROUPS * S5_STATE // S5_BLOCKS
    cin = S5_WIDTH // S5_BLOCKS
    ntiles = S5_GROUPS * S5_STATE // LANES

    @pl.when(pl.program_id(0) == 0)
    def _():
        state_ref[...] = jnp.zeros_like(state_ref)

    for b in range(nb):
        for blk in range(S5_BLOCKS):
            bu = jnp.dot(u_ref[b, :, blk * cin:(blk + 1) * cin], bblk_ref[blk],
                         preferred_element_type=F32)
            sre_ref[b, :, blk * half:(blk + 1) * half] = bu[:, :half]
            sim_ref[b, :, blk * half:(blk + 1) * half] = bu[:, half:]

    for j in range(ntiles):
        lanes = slice(j * LANES, (j + 1) * LANES)
        tabs = [tab_ref[i, :, lanes] for i in range(8)]

        def group(g, carry):
            rows = pl.ds(pl.multiple_of(g * SUBLANES, SUBLANES), SUBLANES)
            out = []
            for b in range(nb):
                xr, xi = sre_ref[b, rows, lanes], sim_ref[b, rows, lanes]
                for stage in range(3):
                    ar, ai = tabs[2 * stage], tabs[2 * stage + 1]
                    sr = pltpu.roll(xr, 1 << stage, axis=0)
                    si = pltpu.roll(xi, 1 << stage, axis=0)
                    xr, xi = xr + (ar * sr - ai * si), xi + (ar * si + ai * sr)
                cr, ci = carry[2 * b], carry[2 * b + 1]
                xr, xi = xr + (tabs[6] * cr - tabs[7] * ci), xi + (tabs[6] * ci + tabs[7] * cr)
                sre_ref[b, rows, lanes] = xr
                sim_ref[b, rows, lanes] = xi
                out.append(jnp.broadcast_to(xr[SUBLANES - 1:, :], xr.shape))
                out.append(jnp.broadcast_to(xi[SUBLANES - 1:, :], xi.shape))
            return tuple(out)

        init = tuple(state_ref[c, :, lanes] for c in range(2 * nb))
        final = lax.fori_loop(0, TC_S5 // SUBLANES, group, init)
        for c in range(2 * nb):
            state_ref[c, :, lanes] = final[c]

    for b in range(nb):
        for blk in range(S5_BLOCKS):
            sre = sre_ref[b, :, blk * half:(blk + 1) * half].astype(BF16)
            sim = sim_ref[b, :, blk * half:(blk + 1) * half].astype(BF16)
            y_ref[:, blk * cin:(blk + 1) * cin] = (
                jnp.dot(sre, cblk_ref[blk, :half, :], preferred_element_type=F32)
                + jnp.dot(sim, cblk_ref[blk, half:, :], preferred_element_type=F32))
        y = y_ref[...] + d_ref[...] * u_ref[b].astype(F32)
        y = jax.nn.gelu(y)
        glu = jnp.dot(y.astype(BF16), wglu_ref[...], preferred_element_type=F32)
        o_ref[b] = (y * _sigmoid(glu)).astype(BF16)


def _s5_params(lam_re, lam_im, b_re, b_im, c_re, c_im, log_step):
    g, p, h = S5_GROUPS, S5_STATE, S5_GROUP
    lr, li = lam_re.astype(F32), lam_im.astype(F32)
    step = jnp.exp(log_step.astype(F32))[:, None]
    mag = jnp.exp(lr * step)
    bar_re, bar_im = mag * jnp.cos(li * step), mag * jnp.sin(li * step)
    den = lr * lr + li * li
    f_re = ((bar_re - 1.0) * lr + bar_im * li) / den
    f_im = (bar_im * lr - (bar_re - 1.0) * li) / den
    f_re, f_im = f_re[:, :, None], f_im[:, :, None]
    bb_re = f_re * b_re.astype(F32) - f_im * b_im.astype(F32)
    bb_im = f_re * b_im.astype(F32) + f_im * b_re.astype(F32)
    gpb = g // S5_BLOCKS
    eye = jnp.eye(gpb, dtype=F32)

    def in_block(m):
        m = m.reshape(S5_BLOCKS, gpb, p, h)
        return jnp.einsum('kgph,gf->kghfp', m, eye).reshape(S5_BLOCKS, gpb * h, gpb * p)

    def out_block(m):
        m = m.reshape(S5_BLOCKS, gpb, h, p)
        return jnp.einsum('kghp,gf->kgpfh', m, eye).reshape(S5_BLOCKS, gpb * p, gpb * h)

    bblk = jnp.concatenate([in_block(bb_re), in_block(bb_im)], axis=2)
    cblk = jnp.concatenate([out_block(c_re.astype(F32)), out_block(-c_im.astype(F32))], axis=1)

    l_re, l_im = bar_re.reshape(1, g * p), bar_im.reshape(1, g * p)
    pows = [(l_re, l_im)]
    for _ in range(SUBLANES - 1):
        q_re, q_im = pows[-1]
        pows.append((q_re * l_re - q_im * l_im, q_re * l_im + q_im * l_re))
    sub = jnp.arange(SUBLANES)[:, None]
    rows = []
    for shift in (1, 2, 4):
        for part in pows[shift - 1]:
            rows.append(jnp.where(sub >= shift, part, 0.0))
    rows.append(jnp.concatenate([q[0] for q in pows], axis=0))
    rows.append(jnp.concatenate([q[1] for q in pows], axis=0))
    return bblk.astype(BF16), cblk.astype(BF16), jnp.stack(rows)


def _s5(u3, bblk, cblk, tab, d_row, wglu_bf):
    b, l, _ = u3.shape
    nstate = S5_GROUPS * S5_STATE
    const = lambda *shape: pl.BlockSpec(shape, lambda i: (0,) * len(shape))
    return pl.pallas_call(
        _s5_kernel,
        grid=(l // TC_S5,),
        in_specs=[
            pl.BlockSpec((b, TC_S5, S5_WIDTH), lambda i: (0, i, 0)),
            const(*bblk.shape), const(*cblk.shape), const(*tab.shape),
            const(1, S5_WIDTH), const(S5_WIDTH, S5_WIDTH),
        ],
        out_specs=pl.BlockSpec((b, TC_S5, S5_WIDTH), lambda i: (0, i, 0)),
        out_shape=jax.ShapeDtypeStruct((b, l, S5_WIDTH), BF16),
        scratch_shapes=[
            pltpu.VMEM((b, TC_S5, nstate), F32),
            pltpu.VMEM((b, TC_S5, nstate), F32),
            pltpu.VMEM((2 * b, SUBLANES, nstate), F32),
            pltpu.VMEM((TC_S5, S5_WIDTH), F32),
        ],
        compiler_params=pltpu.CompilerParams(
            dimension_semantics=("arbitrary",), vmem_limit_bytes=VMEM_LIMIT),
        name="s5_scan",
    )(u3, bblk, cblk, tab, d_row, wglu_bf)


def _merge_kernel(ysb_ref, ys5_ref, gate_ref, x_ref, wsb_ref, ws5_ref, wout_ref, g_ref,
                  wr_ref, br_ref, x1_ref, h2_ref, ek_ref, pk_ref, gk_ref, cnt_ref, carry_ref):
    @pl.when(pl.program_id(0) == 0)
    def _():
        carry_ref[...] = jnp.zeros_like(carry_ref)

    sub = SUB_MERGE
    trow = lax.broadcasted_iota(jnp.int32, (sub, sub), 0)
    tcol = lax.broadcasted_iota(jnp.int32, (sub, sub), 1)
    before = jnp.where(trow < tcol, 1.0, 0.0).astype(BF16)
    erow = lax.broadcasted_iota(jnp.int32, (N_EXPERTS, N_EXPERTS), 0)
    ecol = lax.broadcasted_iota(jnp.int32, (N_EXPERTS, N_EXPERTS), 1)
    lower = jnp.where(ecol < erow, 1.0, 0.0).astype(BF16)
    eidx = lax.broadcasted_iota(jnp.int32, (N_EXPERTS, sub), 0)
    eidx_f = eidx.astype(F32)

    for s in range(x_ref.shape[0] // sub):
        rows = slice(s * sub, (s + 1) * sub)
        a = jnp.dot(ysb_ref[rows, :], wsb_ref[...], preferred_element_type=F32)
        b = jnp.dot(ys5_ref[rows, :], ws5_ref[...], preferred_element_type=F32)
        ga = _sigmoid(gate_ref[rows, :D_MODEL].astype(F32))
        gb = _sigmoid(gate_ref[rows, D_MODEL:].astype(F32))
        merged = (ga * a + gb * b).astype(BF16)
        x1 = x_ref[rows, :] + jnp.dot(merged, wout_ref[...], preferred_element_type=F32)
        x1_ref[rows, :] = x1
        h2 = _rmsnorm(x1, g_ref[...])
        h2_ref[rows, :] = h2

        logits = lax.dot_general(wr_ref[...], h2, (((1,), (1,)), ((), ())),
                                 precision=lax.Precision.HIGHEST,
                                 preferred_element_type=F32) + br_ref[...]
        work = logits
        sel = jnp.zeros(logits.shape, jnp.bool_)
        top = None
        for k in range(TOP_K):
            m = jnp.max(work, axis=0, keepdims=True)
            first = jnp.min(jnp.where(work == m, eidx, N_EXPERTS), axis=0, keepdims=True)
            pick = eidx == first
            sel = jnp.logical_or(sel, pick)
            work = jnp.where(pick, -jnp.inf, work)
            if k == 0:
                top = m
        e = jnp.where(sel, jnp.exp(logits - top), 0.0)
        gate = e / jnp.sum(e, axis=0, keepdims=True)

        mask = jnp.where(sel, 1.0, 0.0)
        pos = jnp.dot(mask.astype(BF16), before, preferred_element_type=F32) + carry_ref[...]
        carry_ref[...] += jnp.sum(mask, axis=1, keepdims=True)

        rank = jnp.dot(lower, mask.astype(BF16), preferred_element_type=F32)
        for k in range(TOP_K):
            ind = jnp.logical_and(sel, rank == float(k))
            pick1 = lambda v: jnp.sum(jnp.where(ind, v, 0.0), axis=0, keepdims=True)
            ek_ref[k:k + 1, rows] = pick1(eidx_f).astype(jnp.int32)
            pk_ref[k:k + 1, rows] = pick1(pos).astype(jnp.int32)
            gk_ref[k:k + 1, rows] = pick1(gate)

    cnt_ref[...] = jnp.broadcast_to(carry_ref[...], cnt_ref.shape)


def _merge(ysb, ys5, gates, x2, wsb, ws5, wout, g_ffn, wr_t, br_col):
    t = x2.shape[0]
    tm = TM_MERGE
    const = lambda *shape: pl.BlockSpec(shape, lambda i: (0,) * len(shape))
    return pl.pallas_call(
        _merge_kernel,
        grid=(t // tm,),
        in_specs=[
            pl.BlockSpec((tm, SB_WIDTH), lambda i: (i, 0)),
            pl.BlockSpec((tm, S5_WIDTH), lambda i: (i, 0)),
            pl.BlockSpec((tm, 2 * D_MODEL), lambda i: (i, 0)),
            pl.BlockSpec((tm, D_MODEL), lambda i: (i, 0)),
            const(SB_WIDTH, D_MODEL), const(S5_WIDTH, D_MODEL), const(D_MODEL, D_MODEL),
            const(1, D_MODEL), const(N_EXPERTS, D_MODEL), const(N_EXPERTS, 1),
        ],
        out_specs=[
            pl.BlockSpec((tm, D_MODEL), lambda i: (i, 0)),
            pl.BlockSpec((tm, D_MODEL), lambda i: (i, 0)),
            pl.BlockSpec((TOP_K, tm), lambda i: (0, i)),
            pl.BlockSpec((TOP_K, tm), lambda i: (0, i)),
            pl.BlockSpec((TOP_K, tm), lambda i: (0, i)),
            pl.BlockSpec((N_EXPERTS, LANES), lambda i: (0, 0)),
        ],
        out_shape=[
            jax.ShapeDtypeStruct((t, D_MODEL), F32),
            jax.ShapeDtypeStruct((t, D_MODEL), F32),
            jax.ShapeDtypeStruct((TOP_K, t), jnp.int32),
            jax.ShapeDtypeStruct((TOP_K, t), jnp.int32),
            jax.ShapeDtypeStruct((TOP_K, t), F32),
            jax.ShapeDtypeStruct((N_EXPERTS, LANES), F32),
        ],
        scratch_shapes=[pltpu.VMEM((N_EXPERTS, 1), F32)],
        compiler_params=pltpu.CompilerParams(
            dimension_semantics=("arbitrary",), vmem_limit_bytes=VMEM_LIMIT),
        name="merge_router",
    )(ysb, ys5, gates, x2, wsb, ws5, wout, g_ffn, wr_t, br_col)


def _layout(ek, pk, t):
    tm = TM_DISPATCH
    ni = t // tm
    ek3, pk3 = ek.reshape(TOP_K, ni, tm), pk.reshape(TOP_K, ni, tm)
    cnt = jnp.stack([jnp.sum(ek3 == e, axis=(0, 2)) for e in range(N_EXPERTS)], axis=1)
    cnt = cnt.astype(jnp.int32)
    seg = (cnt + SUBLANES - 1) // SUBLANES * SUBLANES
    group = jnp.sum(seg, axis=0)
    padded = (group + TR - 1) // TR * TR
    ends = jnp.cumsum(padded)
    dst = (ends - padded)[None, :] + jnp.cumsum(seg, axis=0) - seg
    stage = jnp.cumsum(seg, axis=1) - seg
    base = stage - (jnp.cumsum(cnt, axis=0) - cnt)
    slot = pk3
    for e in range(N_EXPERTS):
        slot = slot + jnp.where(ek3 == e, base[None, :, e, None], 0)
    ntile = (TOP_K * t + ni * N_EXPERTS * (SUBLANES - 1)) // TR + 1 + N_EXPERTS
    nused = ends[-1] // TR
    tile_start = jnp.arange(ntile, dtype=jnp.int32) * TR
    tile_expert = jnp.sum(tile_start[:, None] >= ends[None, :], axis=1).astype(jnp.int32)
    tile_expert = jnp.minimum(tile_expert, jnp.take(tile_expert, nused - 1))
    changes = jnp.concatenate([jnp.zeros((1,), jnp.int32),
                               (tile_expert[1:] != tile_expert[:-1]).astype(jnp.int32)])
    tile_parity = jnp.cumsum(changes) % 2
    tail_tile = jnp.maximum(ends // TR - 1, 0).astype(jnp.int32)
    segs = (stage.reshape(-1), dst.reshape(-1).astype(jnp.int32), seg.reshape(-1))
    return (slot.reshape(TOP_K, t), segs, group.astype(jnp.int32), tail_tile, tile_expert,
            tile_parity.astype(jnp.int32), nused.reshape(1).astype(jnp.int32), ntile)


def _segment_copy(stage_ref, dst_ref, len_ref, idx, staging, buffer, sem, to_buffer):
    aligned = lambda v: pl.multiple_of(v, SUBLANES)
    n = aligned(len_ref[idx])
    a = staging.at[pl.ds(aligned(stage_ref[idx]), n), :]
    b = buffer.at[pl.ds(aligned(dst_ref[idx]), n), :]
    return pltpu.make_async_copy(a, b, sem) if to_buffer else pltpu.make_async_copy(b, a, sem)


def _dispatch_kernel(stage_ref, dst_ref, len_ref, grp_ref, tail_ref, nused_ref, slot_ref, h2_ref,
                     xg_hbm, s_ref, zero_ref, sem, zsem):
    i = pl.program_id(0)

    @pl.when(i == 0)
    def _():
        zero_ref[...] = jnp.zeros_like(zero_ref)
        fill = lambda e: pltpu.make_async_copy(
            zero_ref, xg_hbm.at[pl.ds(pl.multiple_of(tail_ref[e] * TR, TR), TR), :], zsem)
        for e in range(N_EXPERTS):
            @pl.when(grp_ref[e] > 0)
            def _():
                fill(e).start()
        for e in range(N_EXPERTS):
            @pl.when(grp_ref[e] > 0)
            def _():
                fill(e).wait()

        spare = lambda r: pltpu.make_async_copy(
            zero_ref, xg_hbm.at[pl.ds(pl.multiple_of(r * TR, TR), TR), :], zsem)
        ntile = xg_hbm.shape[0] // TR
        lax.fori_loop(nused_ref[0], ntile, lambda r, c: (spare(r).start(), c)[1], 0)
        lax.fori_loop(nused_ref[0], ntile, lambda r, c: (spare(r).wait(), c)[1], 0)

    rowid = lax.broadcasted_iota(jnp.int32, (s_ref.shape[1], h2_ref.shape[0]), 0)
    hit = slot_ref[0:1, :] == rowid
    for k in range(1, TOP_K):
        hit = jnp.logical_or(hit, slot_ref[k:k + 1, :] == rowid)
    onehot = jnp.where(hit, 1.0, 0.0).astype(BF16)
    buf = lax.rem(i, 2)
    s_ref[buf] = jnp.dot(onehot, h2_ref[...].astype(BF16), preferred_element_type=F32)

    def each_segment(tile, buf, action):
        def body(e, carry):
            idx = tile * N_EXPERTS + e

            @pl.when(len_ref[idx] > 0)
            def _():
                action(_segment_copy(stage_ref, dst_ref, len_ref, idx, s_ref.at[buf], xg_hbm,
                                     sem.at[buf], True))
            return carry
        lax.fori_loop(0, N_EXPERTS, body, 0)

    each_segment(i, buf, lambda copy: copy.start())

    @pl.when(i >= 1)
    def _():
        each_segment(i - 1, 1 - buf, lambda copy: copy.wait())

    @pl.when(i == pl.num_programs(0) - 1)
    def _():
        each_segment(i, buf, lambda copy: copy.wait())


def _dispatch(slot, segs, group, tail_tile, nused, h2, nrow):
    t = h2.shape[0]
    tm = TM_DISPATCH
    grid_spec = pltpu.PrefetchScalarGridSpec(
        num_scalar_prefetch=6,
        grid=(t // tm,),
        in_specs=[pl.BlockSpec((TOP_K, tm), lambda i, *_: (0, i)),
                  pl.BlockSpec((tm, D_MODEL), lambda i, *_: (i, 0))],
        out_specs=pl.BlockSpec(memory_space=pl.ANY),
        scratch_shapes=[pltpu.VMEM((2, STAGE_ROWS, D_MODEL), F32), pltpu.VMEM((TR, D_MODEL), F32),
                        pltpu.SemaphoreType.DMA((2,)), pltpu.SemaphoreType.DMA(())],
    )
    return pl.pallas_call(
        _dispatch_kernel,
        grid_spec=grid_spec,
        out_shape=jax.ShapeDtypeStruct((nrow, D_MODEL), F32),
        compiler_params=pltpu.CompilerParams(
            dimension_semantics=("arbitrary",), vmem_limit_bytes=VMEM_LIMIT),
        name="moe_dispatch",
    )(*segs, group, tail_tile, nused, slot, h2)


def _experts_kernel(te_ref, par_ref, nused_ref, xg_ref, wup_ref, bup_ref, wdn_ref, bdn_ref, y_ref,
                    wup_bf, wdn_bf):
    s = pl.program_id(0)
    nused = nused_ref[0]
    r = s - 1

    @pl.when(jnp.logical_and(r >= 0, r < nused))
    def _():
        slot = par_ref[jnp.maximum(r, 0)]
        x = xg_ref[...].astype(BF16)
        hid = jnp.dot(x, wup_bf[slot], preferred_element_type=F32) + bup_ref[0]
        glu = jnp.minimum(hid[:, :D_FF], SWIGLU_LIMIT)
        lin = jnp.clip(hid[:, D_FF:], -SWIGLU_LIMIT, SWIGLU_LIMIT)
        act = glu * _sigmoid(SWIGLU_ALPHA * glu) * (lin + 1.0)
        y_ref[...] = jnp.dot(act.astype(BF16), wdn_bf[slot], preferred_element_type=F32) + bdn_ref[0]

    @pl.when(r >= nused)
    def _():
        y_ref[...] = jnp.zeros_like(y_ref)

    last = te_ref.shape[0] - 1
    cur = jnp.minimum(s, last)
    new_expert = jnp.logical_or(s == 0, te_ref[cur] != te_ref[jnp.maximum(cur - 1, 0)])

    @pl.when(jnp.logical_and(s < nused, new_expert))
    def _():
        slot = par_ref[cur]
        wup_bf[slot] = wup_ref[0].astype(BF16)
        wdn_bf[slot] = wdn_ref[0].astype(BF16)


def _experts(tile_expert, tile_parity, nused, xg, w_up, b_up, w_down, b_down):
    ntile = tile_expert.shape[0]
    ahead = lambda s, te, par, nu: (te[jnp.minimum(s, ntile - 1)], 0, 0)
    current = lambda s, te, par, nu: (te[jnp.maximum(s - 1, 0)], 0, 0)
    by_block = lambda s, te, par, nu: (jnp.maximum(s - 1, 0), 0)
    grid_spec = pltpu.PrefetchScalarGridSpec(
        num_scalar_prefetch=3,
        grid=(ntile + 1,),
        in_specs=[
            pl.BlockSpec((TR, D_MODEL), by_block),
            pl.BlockSpec((1, D_MODEL, 2 * D_FF), ahead),
            pl.BlockSpec((1, 1, 2 * D_FF), current),
            pl.BlockSpec((1, D_FF, D_MODEL), ahead),
            pl.BlockSpec((1, 1, D_MODEL), current),
        ],
        out_specs=pl.BlockSpec((TR, D_MODEL), by_block),
        scratch_shapes=[pltpu.VMEM((2, D_MODEL, 2 * D_FF), BF16),
                        pltpu.VMEM((2, D_FF, D_MODEL), BF16)],
    )
    return pl.pallas_call(
        _experts_kernel,
        grid_spec=grid_spec,
        out_shape=jax.ShapeDtypeStruct(xg.shape, F32),
        compiler_params=pltpu.CompilerParams(
            dimension_semantics=("arbitrary",), vmem_limit_bytes=VMEM_LIMIT),
        name="moe_experts",
    )(tile_expert, tile_parity, nused, xg, w_up, b_up, w_down, b_down)


def _gather_combine_kernel(stage_ref, dst_ref, len_ref, slot_ref, gk_ref, x1_ref, gfin_ref, y_hbm,
                           o_ref, sbuf, sem):
    i = pl.program_id(0)
    nstep = pl.num_programs(0)

    def each_segment(tile, buf, action):
        def body(e, carry):
            idx = tile * N_EXPERTS + e

            @pl.when(len_ref[idx] > 0)
            def _():
                action(_segment_copy(stage_ref, dst_ref, len_ref, idx, sbuf.at[buf], y_hbm,
                                     sem.at[buf], False))
            return carry
        lax.fori_loop(0, N_EXPERTS, body, 0)

    @pl.when(i == 0)
    def _():
        sbuf[...] = jnp.zeros_like(sbuf)
        each_segment(0, 0, lambda copy: copy.start())

    @pl.when(i + 1 < nstep)
    def _():
        each_segment(i + 1, lax.rem(i + 1, 2), lambda copy: copy.start())

    buf = lax.rem(i, 2)
    each_segment(i, buf, lambda copy: copy.wait())

    lane = lax.broadcasted_iota(jnp.int32, (x1_ref.shape[0], sbuf.shape[1]), 1)
    weight = jnp.where(slot_ref[0] == lane, gk_ref[0], 0.0)
    for k in range(1, TOP_K):
        weight = weight + jnp.where(slot_ref[k] == lane, gk_ref[k], 0.0)
    hi = weight.astype(BF16)
    lo = (weight - hi.astype(F32)).astype(BF16)
    rows = sbuf[buf].astype(BF16)
    acc = (x1_ref[...] + jnp.dot(hi, rows, preferred_element_type=F32)
           + jnp.dot(lo, rows, preferred_element_type=F32))
    o_ref[...] = _rmsnorm(acc, gfin_ref[...])


def _gather_combine(slot_col, segs, x1, gk_col, g_fin, y):
    t = x1.shape[0]
    tm = TM_DISPATCH
    grid_spec = pltpu.PrefetchScalarGridSpec(
        num_scalar_prefetch=3,
        grid=(t // tm,),
        in_specs=[
            pl.BlockSpec((TOP_K, tm, 1), lambda i, *_: (0, i, 0)),
            pl.BlockSpec((TOP_K, tm, 1), lambda i, *_: (0, i, 0)),
            pl.BlockSpec((tm, D_MODEL), lambda i, *_: (i, 0)),
            pl.BlockSpec((1, D_MODEL), lambda i, *_: (0, 0)),
            pl.BlockSpec(memory_space=pl.ANY),
        ],
        out_specs=pl.BlockSpec((tm, D_MODEL), lambda i, *_: (i, 0)),
        scratch_shapes=[pltpu.VMEM((2, STAGE_ROWS, D_MODEL), F32), pltpu.SemaphoreType.DMA((2,))],
    )
    return pl.pallas_call(
        _gather_combine_kernel,
        grid_spec=grid_spec,
        out_shape=jax.ShapeDtypeStruct((t, D_MODEL), F32),
        compiler_params=pltpu.CompilerParams(
            dimension_semantics=("arbitrary",), vmem_limit_bytes=VMEM_LIMIT),
        name="gather_combine_norm",
    )(*segs, slot_col, gk_col, x1, g_fin, y)


def kernel(x, norm_mix, w_in, s5_lam_re, s5_lam_im, s5_b_re, s5_b_im, s5_c_re, s5_c_im, s5_d,
           s5_log_step, w_glu, w_branch_sb, w_branch_s5, w_out, norm_ffn, w_router, b_router,
           w_up, b_up, w_down, b_down, norm_final):
    bsz, seq, d = x.shape
    t = bsz * seq
    assert norm_mix.shape[0] == 1, "single-layer block"
    l = 0
    x2 = x.reshape(t, d).astype(F32)
    qkv, u, gates = _inproj(x2, norm_mix[l].reshape(1, d).astype(F32), w_in[l].astype(BF16))
    ysb = _attention(qkv.reshape(bsz, seq, 3 * SB_WIDTH))
    bblk, cblk, tab = _s5_params(s5_lam_re[l], s5_lam_im[l], s5_b_re[l], s5_b_im[l],
                                 s5_c_re[l], s5_c_im[l], s5_log_step[l])
    ys5 = _s5(u.reshape(bsz, seq, S5_WIDTH), bblk, cblk, tab,
              s5_d[l].reshape(1, S5_WIDTH).astype(F32), w_glu[l].astype(BF16))
    x1, h2, ek, pk, gk, cnt = _merge(
        ysb.reshape(t, SB_WIDTH), ys5.reshape(t, S5_WIDTH), gates, x2,
        w_branch_sb[l].astype(BF16), w_branch_s5[l].astype(BF16), w_out[l].astype(BF16),
        norm_ffn[l].reshape(1, d).astype(F32), w_router[l].T.astype(F32),
        b_router[l].reshape(N_EXPERTS, 1).astype(F32))
    slot, segs, group, tail_tile, tile_expert, tile_parity, nused, ntile = _layout(ek, pk, t)
    xg = _dispatch(slot, segs, group, tail_tile, nused, h2, ntile * TR)
    y = _experts(tile_expert, tile_parity, nused, xg, w_up[l],
                 b_up[l].reshape(N_EXPERTS, 1, 2 * D_FF), w_down[l],
                 b_down[l].reshape(N_EXPERTS, 1, D_MODEL))
    out = _gather_combine(slot.reshape(TOP_K, t, 1), segs, x1, gk.reshape(TOP_K, t, 1),
                          norm_final.reshape(1, d).astype(F32), y)
    return out.reshape(bsz, seq, d).astype(x.dtype)
```

```python
import jax
import jax.numpy as jnp
from jax import lax
from jax.experimental import pallas as pl
from jax.experimental.pallas import tpu as pltpu

F32 = jnp.float32
BF16 = jnp.bfloat16

D_MODEL = 1024
SB_HEADS = 8
SB_HEAD_DIM = 64
SB_WIDTH = SB_HEADS * SB_HEAD_DIM
S5_WIDTH = 512
S5_GROUP = 16
S5_GROUPS = 32
S5_STATE = 64
N_EXPERTS = 32
TOP_K = 4
D_FF = 1024
SWIGLU_LIMIT = 7.0
SWIGLU_ALPHA = 1.702
RMS_EPS = 1e-5
OFF_U = 3 * SB_WIDTH
OFF_GATES = OFF_U + S5_WIDTH
IN_WIDTH = OFF_GATES + 2 * D_MODEL

LANES = 128
SUBLANES = 8
VMEM_LIMIT = 56 * 1024 * 1024

TM_PROJ = 512
TQ = 256
TK = 256
Q_PER_STEP = 2
EXP_UNDERFLOW = 110.0
TC_S5 = 256
S5_BLOCKS = 4
TM_MERGE = 512
SUB_MERGE = 512
TR = 512
TM_DISPATCH = 256
STAGE_ROWS = TOP_K * TM_DISPATCH + N_EXPERTS * SUBLANES


def _rmsnorm(x, g):
    return x * lax.rsqrt(jnp.mean(x * x, axis=-1, keepdims=True) + RMS_EPS) * g


def _sigmoid(x):
    return 1.0 / (1.0 + jnp.exp(-x))


def _inproj_kernel(x_ref, g_ref, w_ref, qkv_ref, u_ref, gate_ref):
    hb = _rmsnorm(x_ref[...], g_ref[...]).astype(BF16)
    chunk = SB_WIDTH

    def proj(c0):
        return jnp.dot(hb, w_ref[:, c0:c0 + chunk], preferred_element_type=F32)

    qkv_ref[:, 0:chunk] = (proj(0) * (SB_HEAD_DIM ** -0.5)).astype(BF16)
    qkv_ref[:, chunk:2 * chunk] = proj(chunk).astype(BF16)
    qkv_ref[:, 2 * chunk:3 * chunk] = proj(2 * chunk).astype(BF16)
    u_ref[...] = proj(OFF_U).astype(BF16)
    for c in range(2 * D_MODEL // chunk):
        gate_ref[:, c * chunk:(c + 1) * chunk] = proj(OFF_GATES + c * chunk).astype(BF16)


def _inproj(x2, g, w_bf):
    t = x2.shape[0]
    return pl.pallas_call(
        _inproj_kernel,
        grid=(t // TM_PROJ,),
        in_specs=[
            pl.BlockSpec((TM_PROJ, D_MODEL), lambda i: (i, 0)),
            pl.BlockSpec((1, D_MODEL), lambda i: (0, 0)),
            pl.BlockSpec((D_MODEL, IN_WIDTH), lambda i: (0, 0)),
        ],
        out_specs=[
            pl.BlockSpec((TM_PROJ, 3 * SB_WIDTH), lambda i: (i, 0)),
            pl.BlockSpec((TM_PROJ, S5_WIDTH), lambda i: (i, 0)),
            pl.BlockSpec((TM_PROJ, 2 * D_MODEL), lambda i: (i, 0)),
        ],
        out_shape=[
            jax.ShapeDtypeStruct((t, 3 * SB_WIDTH), BF16),
            jax.ShapeDtypeStruct((t, S5_WIDTH), BF16),
            jax.ShapeDtypeStruct((t, 2 * D_MODEL), BF16),
        ],
        compiler_params=pltpu.CompilerParams(
            dimension_semantics=("parallel",), vmem_limit_bytes=VMEM_LIMIT),
        name="inproj",
    )(x2, g, w_bf)


def _attn_kernel(q_ref, k_ref, v_ref, o_ref, acc_ref, cb_ref):
    step = pl.program_id(2)
    nheads = LANES // SB_HEAD_DIM
    is_h0 = lax.broadcasted_iota(jnp.int32, (TQ, LANES), 1) < SB_HEAD_DIM
    row = lax.broadcasted_iota(jnp.int32, (TQ, TK), 0)
    col = lax.broadcasted_iota(jnp.int32, (TQ, TK), 1)
    tri = jnp.where(row >= col, 1.0, 0.0).astype(BF16)
    causal = col < row
    acc_ref[...] = jnp.zeros_like(acc_ref)
    cb_ref[...] = jnp.zeros_like(cb_ref)

    def scores(qm, k, mask):
        z = lax.dot_general(qm, k, (((1,), (1,)), ((), ())), preferred_element_type=F32)
        log_keep = -(jnp.maximum(z, 0.0) + jnp.log(1.0 + jnp.exp(-jnp.abs(z))))
        if mask is not None:
            log_keep = jnp.where(mask, log_keep, 0.0)
        hi = log_keep.astype(BF16)
        lo = (log_keep - hi.astype(F32)).astype(BF16)
        rcum = (jnp.dot(hi, tri, preferred_element_type=F32)
                + jnp.dot(lo, tri, preferred_element_type=F32))
        return z, rcum

    def pair(sub, qms, kt, diag):
        has_b = kt >= 1
        ka = pl.multiple_of(kt * TK, TK)
        kb = pl.multiple_of(jnp.maximum(kt - 1, 0) * TK, TK)
        k_a, v_a = k_ref[0, pl.ds(ka, TK), :], v_ref[0, pl.ds(ka, TK), :]
        k_b, v_b = k_ref[0, pl.ds(kb, TK), :], v_ref[0, pl.ds(kb, TK), :]
        mask_a = causal if diag else None
        pvs, worst = [], None
        for h, qm in enumerate(qms):
            z_a, rcum_a = scores(qm, k_a, mask_a)
            z_b, rcum_b = scores(qm, k_b, has_b)
            cb = cb_ref[nheads * sub + h]
            cb_a = cb + rcum_a[:, 0:1]
            w_a = jnp.exp(z_a + rcum_a + cb)
            if diag:
                w_a = jnp.where(causal, w_a, 0.0)
            w_b = jnp.where(has_b, jnp.exp(z_b + rcum_b + cb_a), 0.0)
            pvs.append(jnp.dot(w_a.astype(BF16), v_a, preferred_element_type=F32)
                       + jnp.dot(w_b.astype(BF16), v_b, preferred_element_type=F32))
            cb_new = cb_a + rcum_b[:, 0:1]
            cb_ref[nheads * sub + h] = cb_new
            top = jnp.max(cb_new)
            worst = top if worst is None else jnp.maximum(worst, top)
        acc_ref[sub] += jnp.where(is_h0, pvs[0], pvs[1])
        return jnp.logical_not(worst < -EXP_UNDERFLOW)

    tiles = []
    for sub in range(Q_PER_STEP):
        q = q_ref[0, sub * TQ:(sub + 1) * TQ, :]
        zero = jnp.zeros_like(q)
        qms = (jnp.where(is_h0, q, zero), jnp.where(is_h0, zero, q))
        qi = step * Q_PER_STEP + sub
        tiles.append((qms, qi, pair(sub, qms, qi, True)))

    for sub, (qms, qi, live) in enumerate(tiles):
        def cond(carry):
            kt, live = carry
            return jnp.logical_and(kt >= 0, live)

        def body(carry, sub=sub, qms=qms):
            kt, _ = carry
            return kt - 2, pair(sub, qms, kt, False)

        lax.while_loop(cond, body, (qi - 2, live))
        o_ref[0, sub * TQ:(sub + 1) * TQ, :] = acc_ref[sub].astype(BF16)


def _attention(qkv3):
    b, l, _ = qkv3.shape
    pairs = SB_WIDTH // LANES
    tq = Q_PER_STEP * TQ
    nheads = LANES // SB_HEAD_DIM
    return pl.pallas_call(
        _attn_kernel,
        grid=(b, pairs, l // tq),
        in_specs=[
            pl.BlockSpec((1, tq, LANES), lambda bi, hp, qi: (bi, qi, hp)),
            pl.BlockSpec((1, l, LANES), lambda bi, hp, qi: (bi, 0, pairs + hp)),
            pl.BlockSpec((1, l, LANES), lambda bi, hp, qi: (bi, 0, 2 * pairs + hp)),
        ],
        out_specs=pl.BlockSpec((1, tq, LANES), lambda bi, hp, qi: (bi, qi, hp)),
        out_shape=jax.ShapeDtypeStruct((b, l, SB_WIDTH), BF16),
        scratch_shapes=[pltpu.VMEM((Q_PER_STEP, TQ, LANES), F32),
                        pltpu.VMEM((Q_PER_STEP * nheads, TQ, 1), F32)],
        compiler_params=pltpu.CompilerParams(
            dimension_semantics=("parallel", "parallel", "arbitrary"),
            vmem_limit_bytes=VMEM_LIMIT),
        name="sb_attention",
    )(qkv3, qkv3, qkv3)


def _s5_kernel(u_ref, bblk_ref, cblk_ref, tab_ref, d_ref, wglu_ref, o_ref,
               sre_ref, sim_ref, state_ref, y_ref):
    nb = u_ref.shape[0]
    half = S5_GROUPS * S5_STATE // S5_BLOCKS
    cin = S5_WIDTH // S5_BLOCKS
    ntiles = S5_GROUPS * S5_STATE // LANES

    @pl.when(pl.program_id(0) == 0)
    def _():
        state_ref[...] = jnp.zeros_like(state_ref)

    for b in range(nb):
        for blk in range(S5_BLOCKS):
            bu = jnp.dot(u_ref[b, :, blk * cin:(blk + 1) * cin], bblk_ref[blk],
                         preferred_element_type=F32)
            sre_ref[b, :, blk * half:(blk + 1) * half] = bu[:, :half]
            sim_ref[b, :, blk * half:(blk + 1) * half] = bu[:, half:]

    for j in range(ntiles):
        lanes = slice(j * LANES, (j + 1) * LANES)
        tabs = [tab_ref[i, :, lanes] for i in range(8)]

        def group(g, carry):
            rows = pl.ds(pl.multiple_of(g * SUBLANES, SUBLANES), SUBLANES)
            out = []
            for b in range(nb):
                xr, xi = sre_ref[b, rows, lanes], sim_ref[b, rows, lanes]
                for stage in range(3):
                    ar, ai = tabs[2 * stage], tabs[2 * stage + 1]
                    sr = pltpu.roll(xr, 1 << stage, axis=0)
                    si = pltpu.roll(xi, 1 << stage, axis=0)
                    xr, xi = xr + (ar * sr - ai * si), xi + (ar * si + ai * sr)
                cr, ci = carry[2 * b], carry[2 * b + 1]
                xr, xi = xr + (tabs[6] * cr - tabs[7] * ci), xi + (tabs[6] * ci + tabs[7] * cr)
                sre_ref[b, rows, lanes] = xr
                sim_ref[b, rows, lanes] = xi
                out.append(jnp.broadcast_to(xr[SUBLANES - 1:, :], xr.shape))
                out.append(jnp.broadcast_to(xi[SUBLANES - 1:, :], xi.shape))
            return tuple(out)

        init = tuple(state_ref[c, :, lanes] for c in range(2 * nb))
        final = lax.fori_loop(0, TC_S5 // SUBLANES, group, init)
        for c in range(2 * nb):
            state_ref[c, :, lanes] = final[c]

    for b in range(nb):
        for blk in range(S5_BLOCKS):
            sre = sre_ref[b, :, blk * half:(blk + 1) * half].astype(BF16)
            sim = sim_ref[b, :, blk * half:(blk + 1) * half].astype(BF16)
            y_ref[:, blk * cin:(blk + 1) * cin] = (
                jnp.dot(sre, cblk_ref[blk, :half, :], preferred_element_type=F32)
                + jnp.dot(sim, cblk_ref[blk, half:, :], preferred_element_type=F32))
        y = y_ref[...] + d_ref[...] * u_ref[b].astype(F32)
        y = jax.nn.gelu(y)
        glu = jnp.dot(y.astype(BF16), wglu_ref[...], preferred_element_type=F32)
        o_ref[b] = (y * _sigmoid(glu)).astype(BF16)


def _s5_params(lam_re, lam_im, b_re, b_im, c_re, c_im, log_step):
    g, p, h = S5_GROUPS, S5_STATE, S5_GROUP
    lr, li = lam_re.astype(F32), lam_im.astype(F32)
    step = jnp.exp(log_step.astype(F32))[:, None]
    mag = jnp.exp(lr * step)
    bar_re, bar_im = mag * jnp.cos(li * step), mag * jnp.sin(li * step)
    den = lr * lr + li * li
    f_re = ((bar_re - 1.0) * lr + bar_im * li) / den
    f_im = (bar_im * lr - (bar_re - 1.0) * li) / den
    f_re, f_im = f_re[:, :, None], f_im[:, :, None]
    bb_re = f_re * b_re.astype(F32) - f_im * b_im.astype(F32)
    bb_im = f_re * b_im.astype(F32) + f_im * b_re.astype(F32)
    gpb = g // S5_BLOCKS
    eye = jnp.eye(gpb, dtype=F32)

    def in_block(m):
        m = m.reshape(S5_BLOCKS, gpb, p, h)
        return jnp.einsum('kgph,gf->kghfp', m, eye).reshape(S5_BLOCKS, gpb * h, gpb * p)

    def out_block(m):
        m = m.reshape(S5_BLOCKS, gpb, h, p)
        return jnp.einsum('kghp,gf->kgpfh', m, eye).reshape(S5_BLOCKS, gpb * p, gpb * h)

    bblk = jnp.concatenate([in_block(bb_re), in_block(bb_im)], axis=2)
    cblk = jnp.concatenate([out_block(c_re.astype(F32)), out_block(-c_im.astype(F32))], axis=1)

    l_re, l_im = bar_re.reshape(1, g * p), bar_im.reshape(1, g * p)
    pows = [(l_re, l_im)]
    for _ in range(SUBLANES - 1):
        q_re, q_im = pows[-1]
        pows.append((q_re * l_re - q_im * l_im, q_re * l_im + q_im * l_re))
    sub = jnp.arange(SUBLANES)[:, None]
    rows = []
    for shift in (1, 2, 4):
        for part in pows[shift - 1]:
            rows.append(jnp.where(sub >= shift, part, 0.0))
    rows.append(jnp.concatenate([q[0] for q in pows], axis=0))
    rows.append(jnp.concatenate([q[1] for q in pows], axis=0))
    return bblk.astype(BF16), cblk.astype(BF16), jnp.stack(rows)


def _s5(u3, bblk, cblk, tab, d_row, wglu_bf):
    b, l, _ = u3.shape
    nstate = S5_GROUPS * S5_STATE
    const = lambda *shape: pl.BlockSpec(shape, lambda i: (0,) * len(shape))
    return pl.pallas_call(
        _s5_kernel,
        grid=(l // TC_S5,),
        in_specs=[
            pl.BlockSpec((b, TC_S5, S5_WIDTH), lambda i: (0, i, 0)),
            const(*bblk.shape), const(*cblk.shape), const(*tab.shape),
            const(1, S5_WIDTH), const(S5_WIDTH, S5_WIDTH),
        ],
        out_specs=pl.BlockSpec((b, TC_S5, S5_WIDTH), lambda i: (0, i, 0)),
        out_shape=jax.ShapeDtypeStruct((b, l, S5_WIDTH), BF16),
        scratch_shapes=[
            pltpu.VMEM((b, TC_S5, nstate), F32),
            pltpu.VMEM((b, TC_S5, nstate), F32),
            pltpu.VMEM((2 * b, SUBLANES, nstate), F32),
            pltpu.VMEM((TC_S5, S5_WIDTH), F32),
        ],
        compiler_params=pltpu.CompilerParams(
            dimension_semantics=("arbitrary",), vmem_limit_bytes=VMEM_LIMIT),
        name="s5_scan",
    )(u3, bblk, cblk, tab, d_row, wglu_bf)


def _merge_kernel(ysb_ref, ys5_ref, gate_ref, x_ref, wsb_ref, ws5_ref, wout_ref, g_ref,
                  wr_ref, br_ref, x1_ref, h2_ref, ek_ref, pk_ref, gk_ref, cnt_ref, carry_ref):
    @pl.when(pl.program_id(0) == 0)
    def _():
        carry_ref[...] = jnp.zeros_like(carry_ref)

    sub = SUB_MERGE
    trow = lax.broadcasted_iota(jnp.int32, (sub, sub), 0)
    tcol = lax.broadcasted_iota(jnp.int32, (sub, sub), 1)
    before = jnp.where(trow < tcol, 1.0, 0.0).astype(BF16)
    erow = lax.broadcasted_iota(jnp.int32, (N_EXPERTS, N_EXPERTS), 0)
    ecol = lax.broadcasted_iota(jnp.int32, (N_EXPERTS, N_EXPERTS), 1)
    lower = jnp.where(ecol < erow, 1.0, 0.0).astype(BF16)
    eidx = lax.broadcasted_iota(jnp.int32, (N_EXPERTS, sub), 0)
    eidx_f = eidx.astype(F32)

    for s in range(x_ref.shape[0] // sub):
        rows = slice(s * sub, (s + 1) * sub)
        a = jnp.dot(ysb_ref[rows, :], wsb_ref[...], preferred_element_type=F32)
        b = jnp.dot(ys5_ref[rows, :], ws5_ref[...], preferred_element_type=F32)
        ga = _sigmoid(gate_ref[rows, :D_MODEL].astype(F32))
        gb = _sigmoid(gate_ref[rows, D_MODEL:].astype(F32))
        merged = (ga * a + gb * b).astype(BF16)
        x1 = x_ref[rows, :] + jnp.dot(merged, wout_ref[...], preferred_element_type=F32)
        x1_ref[rows, :] = x1
        h2 = _rmsnorm(x1, g_ref[...])
        h2_ref[rows, :] = h2

        logits = lax.dot_general(wr_ref[...], h2, (((1,), (1,)), ((), ())),
                                 precision=lax.Precision.HIGHEST,
                                 preferred_element_type=F32) + br_ref[...]
        work = logits
        sel = jnp.zeros(logits.shape, jnp.bool_)
        top = None
        for k in range(TOP_K):
            m = jnp.max(work, axis=0, keepdims=True)
            first = jnp.min(jnp.where(work == m, eidx, N_EXPERTS), axis=0, keepdims=True)
            pick = eidx == first
            sel = jnp.logical_or(sel, pick)
            work = jnp.where(pick, -jnp.inf, work)
            if k == 0:
                top = m
        e = jnp.where(sel, jnp.exp(logits - top), 0.0)
        gate = e / jnp.sum(e, axis=0, keepdims=True)

        mask = jnp.where(sel, 1.0, 0.0)
        pos = jnp.dot(mask.astype(BF16), before, preferred_element_type=F32) + carry_ref[...]
        carry_ref[...] += jnp.sum(mask, axis=1, keepdims=True)

        rank = jnp.dot(lower, mask.astype(BF16), preferred_element_type=F32)
        for k in range(TOP_K):
            ind = jnp.logical_and(sel, rank == float(k))
            pick1 = lambda v: jnp.sum(jnp.where(ind, v, 0.0), axis=0, keepdims=True)
            ek_ref[k:k + 1, rows] = pick1(eidx_f).astype(jnp.int32)
            pk_ref[k:k + 1, rows] = pick1(pos).astype(jnp.int32)
            gk_ref[k:k + 1, rows] = pick1(gate)

    cnt_ref[...] = jnp.broadcast_to(carry_ref[...], cnt_ref.shape)


def _merge(ysb, ys5, gates, x2, wsb, ws5, wout, g_ffn, wr_t, br_col):
    t = x2.shape[0]
    tm = TM_MERGE
    const = lambda *shape: pl.BlockSpec(shape, lambda i: (0,) * len(shape))
    return pl.pallas_call(
        _merge_kernel,
        grid=(t // tm,),
        in_specs=[
            pl.BlockSpec((tm, SB_WIDTH), lambda i: (i, 0)),
            pl.BlockSpec((tm, S5_WIDTH), lambda i: (i, 0)),
            pl.BlockSpec((tm, 2 * D_MODEL), lambda i: (i, 0)),
            pl.BlockSpec((tm, D_MODEL), lambda i: (i, 0)),
            const(SB_WIDTH, D_MODEL), const(S5_WIDTH, D_MODEL), const(D_MODEL, D_MODEL),
            const(1, D_MODEL), const(N_EXPERTS, D_MODEL), const(N_EXPERTS, 1),
        ],
        out_specs=[
            pl.BlockSpec((tm, D_MODEL), lambda i: (i, 0)),
            pl.BlockSpec((tm, D_MODEL), lambda i: (i, 0)),
            pl.BlockSpec((TOP_K, tm), lambda i: (0, i)),
            pl.BlockSpec((TOP_K, tm), lambda i: (0, i)),
            pl.BlockSpec((TOP_K, tm), lambda i: (0, i)),
            pl.BlockSpec((N_EXPERTS, LANES), lambda i: (0, 0)),
        ],
        out_shape=[
            jax.ShapeDtypeStruct((t, D_MODEL), F32),
            jax.ShapeDtypeStruct((t, D_MODEL), F32),
            jax.ShapeDtypeStruct((TOP_K, t), jnp.int32),
            jax.ShapeDtypeStruct((TOP_K, t), jnp.int32),
            jax.ShapeDtypeStruct((TOP_K, t), F32),
            jax.ShapeDtypeStruct((N_EXPERTS, LANES), F32),
        ],
        scratch_shapes=[pltpu.VMEM((N_EXPERTS, 1), F32)],
        compiler_params=pltpu.CompilerParams(
            dimension_semantics=("arbitrary",), vmem_limit_bytes=VMEM_LIMIT),
        name="merge_router",
    )(ysb, ys5, gates, x2, wsb, ws5, wout, g_ffn, wr_t, br_col)


def _layout(ek, pk, t):
    tm = TM_DISPATCH
    ni = t // tm
    ek3, pk3 = ek.reshape(TOP_K, ni, tm), pk.reshape(TOP_K, ni, tm)
    cnt = jnp.stack([jnp.sum(ek3 == e, axis=(0, 2)) for e in range(N_EXPERTS)], axis=1)
    cnt = cnt.astype(jnp.int32)
    seg = (cnt + SUBLANES - 1) // SUBLANES * SUBLANES
    group = jnp.sum(seg, axis=0)
    padded = (group + TR - 1) // TR * TR
    ends = jnp.cumsum(padded)
    dst = (ends - padded)[None, :] + jnp.cumsum(seg, axis=0) - seg
    stage = jnp.cumsum(seg, axis=1) - seg
    base = stage - (jnp.cumsum(cnt, axis=0) - cnt)
    slot = pk3
    for e in range(N_EXPERTS):
        slot = slot + jnp.where(ek3 == e, base[None, :, e, None], 0)
    ntile = (TOP_K * t + ni * N_EXPERTS * (SUBLANES - 1)) // TR + 1 + N_EXPERTS
    nused = ends[-1] // TR
    tile_start = jnp.arange(ntile, dtype=jnp.int32) * TR
    tile_expert = jnp.sum(tile_start[:, None] >= ends[None, :], axis=1).astype(jnp.int32)
    tile_expert = jnp.minimum(tile_expert, jnp.take(tile_expert, nused - 1))
    changes = jnp.concatenate([jnp.zeros((1,), jnp.int32),
                               (tile_expert[1:] != tile_expert[:-1]).astype(jnp.int32)])
    tile_parity = jnp.cumsum(changes) % 2
    tail_tile = jnp.maximum(ends // TR - 1, 0).astype(jnp.int32)
    segs = (stage.reshape(-1), dst.reshape(-1).astype(jnp.int32), seg.reshape(-1))
    return (slot.reshape(TOP_K, t), segs, group.astype(jnp.int32), tail_tile, tile_expert,
            tile_parity.astype(jnp.int32), nused.reshape(1).astype(jnp.int32), ntile)


def _segment_copy(stage_ref, dst_ref, len_ref, idx, staging, buffer, sem, to_buffer):
    aligned = lambda v: pl.multiple_of(v, SUBLANES)
    n = aligned(len_ref[idx])
    a = staging.at[pl.ds(aligned(stage_ref[idx]), n), :]
    b = buffer.at[pl.ds(aligned(dst_ref[idx]), n), :]
    return pltpu.make_async_copy(a, b, sem) if to_buffer else pltpu.make_async_copy(b, a, sem)


def _dispatch_kernel(stage_ref, dst_ref, len_ref, grp_ref, tail_ref, nused_ref, slot_ref, h2_ref,
                     xg_hbm, s_ref, zero_ref, sem, zsem):
    i = pl.program_id(0)

    @pl.when(i == 0)
    def _():
        zero_ref[...] = jnp.zeros_like(zero_ref)
        fill = lambda e: pltpu.make_async_copy(
            zero_ref, xg_hbm.at[pl.ds(pl.multiple_of(tail_ref[e] * TR, TR), TR), :], zsem)
        for e in range(N_EXPERTS):
            @pl.when(grp_ref[e] > 0)
            def _():
                fill(e).start()
        for e in range(N_EXPERTS):
            @pl.when(grp_ref[e] > 0)
            def _():
                fill(e).wait()

        spare = lambda r: pltpu.make_async_copy(
            zero_ref, xg_hbm.at[pl.ds(pl.multiple_of(r * TR, TR), TR), :], zsem)
        ntile = xg_hbm.shape[0] // TR
        lax.fori_loop(nused_ref[0], ntile, lambda r, c: (spare(r).start(), c)[1], 0)
        lax.fori_loop(nused_ref[0], ntile, lambda r, c: (spare(r).wait(), c)[1], 0)

    rowid = lax.broadcasted_iota(jnp.int32, (s_ref.shape[1], h2_ref.shape[0]), 0)
    hit = slot_ref[0:1, :] == rowid
    for k in range(1, TOP_K):
        hit = jnp.logical_or(hit, slot_ref[k:k + 1, :] == rowid)
    onehot = jnp.where(hit, 1.0, 0.0).astype(BF16)
    buf = lax.rem(i, 2)
    s_ref[buf] = jnp.dot(onehot, h2_ref[...].astype(BF16), preferred_element_type=F32)

    def each_segment(tile, buf, action):
        def body(e, carry):
            idx = tile * N_EXPERTS + e

            @pl.when(len_ref[idx] > 0)
            def _():
                action(_segment_copy(stage_ref, dst_ref, len_ref, idx, s_ref.at[buf], xg_hbm,
                                     sem.at[buf], True))
            return carry
        lax.fori_loop(0, N_EXPERTS, body, 0)

    each_segment(i, buf, lambda copy: copy.start())

    @pl.when(i >= 1)
    def _():
        each_segment(i - 1, 1 - buf, lambda copy: copy.wait())

    @pl.when(i == pl.num_programs(0) - 1)
    def _():
        each_segment(i, buf, lambda copy: copy.wait())


def _dispatch(slot, segs, group, tail_tile, nused, h2, nrow):
    t = h2.shape[0]
    tm = TM_DISPATCH
    grid_spec = pltpu.PrefetchScalarGridSpec(
        num_scalar_prefetch=6,
        grid=(t // tm,),
        in_specs=[pl.BlockSpec((TOP_K, tm), lambda i, *_: (0, i)),
                  pl.BlockSpec((tm, D_MODEL), lambda i, *_: (i, 0))],
        out_specs=pl.BlockSpec(memory_space=pl.ANY),
        scratch_shapes=[pltpu.VMEM((2, STAGE_ROWS, D_MODEL), F32), pltpu.VMEM((TR, D_MODEL), F32),
                        pltpu.SemaphoreType.DMA((2,)), pltpu.SemaphoreType.DMA(())],
    )
    return pl.pallas_call(
        _dispatch_kernel,
        grid_spec=grid_spec,
        out_shape=jax.ShapeDtypeStruct((nrow, D_MODEL), F32),
        compiler_params=pltpu.CompilerParams(
            dimension_semantics=("arbitrary",), vmem_limit_bytes=VMEM_LIMIT),
        name="moe_dispatch",
    )(*segs, group, tail_tile, nused, slot, h2)


def _experts_kernel(te_ref, par_ref, nused_ref, xg_ref, wup_ref, bup_ref, wdn_ref, bdn_ref, y_ref,
                    wup_bf, wdn_bf):
    s = pl.program_id(0)
    nused = nused_ref[0]
    r = s - 1

    @pl.when(jnp.logical_and(r >= 0, r < nused))
    def _():
        slot = par_ref[jnp.maximum(r, 0)]
        x = xg_ref[...].astype(BF16)
        hid = jnp.dot(x, wup_bf[slot], preferred_element_type=F32) + bup_ref[0]
        glu = jnp.minimum(hid[:, :D_FF], SWIGLU_LIMIT)
        lin = jnp.clip(hid[:, D_FF:], -SWIGLU_LIMIT, SWIGLU_LIMIT)
        act = glu * _sigmoid(SWIGLU_ALPHA * glu) * (lin + 1.0)
        y_ref[...] = jnp.dot(act.astype(BF16), wdn_bf[slot], preferred_element_type=F32) + bdn_ref[0]

    @pl.when(r >= nused)
    def _():
        y_ref[...] = jnp.zeros_like(y_ref)

    last = te_ref.shape[0] - 1
    cur = jnp.minimum(s, last)
    new_expert = jnp.logical_or(s == 0, te_ref[cur] != te_ref[jnp.maximum(cur - 1, 0)])

    @pl.when(jnp.logical_and(s < nused, new_expert))
    def _():
        slot = par_ref[cur]
        wup_bf[slot] = wup_ref[0].astype(BF16)
        wdn_bf[slot] = wdn_ref[0].astype(BF16)


def _experts(tile_expert, tile_parity, nused, xg, w_up, b_up, w_down, b_down):
    ntile = tile_expert.shape[0]
    ahead = lambda s, te, par, nu: (te[jnp.minimum(s, ntile - 1)], 0, 0)
    current = lambda s, te, par, nu: (te[jnp.maximum(s - 1, 0)], 0, 0)
    by_block = lambda s, te, par, nu: (jnp.maximum(s - 1, 0), 0)
    grid_spec = pltpu.PrefetchScalarGridSpec(
        num_scalar_prefetch=3,
        grid=(ntile + 1,),
        in_specs=[
            pl.BlockSpec((TR, D_MODEL), by_block),
            pl.BlockSpec((1, D_MODEL, 2 * D_FF), ahead),
            pl.BlockSpec((1, 1, 2 * D_FF), current),
            pl.BlockSpec((1, D_FF, D_MODEL), ahead),
            pl.BlockSpec((1, 1, D_MODEL), current),
        ],
        out_specs=pl.BlockSpec((TR, D_MODEL), by_block),
        scratch_shapes=[pltpu.VMEM((2, D_MODEL, 2 * D_FF), BF16),
                        pltpu.VMEM((2, D_FF, D_MODEL), BF16)],
    )
    return pl.pallas_call(
        _experts_kernel,
        grid_spec=grid_spec,
        out_shape=jax.ShapeDtypeStruct(xg.shape, F32),
        compiler_params=pltpu.CompilerParams(
            dimension_semantics=("arbitrary",), vmem_limit_bytes=VMEM_LIMIT),
        name="moe_experts",
    )(tile_expert, tile_parity, nused, xg, w_up, b_up, w_down, b_down)


def _gather_combine_kernel(stage_ref, dst_ref, len_ref, slot_ref, gk_ref, x1_ref, gfin_ref, y_hbm,
                           o_ref, sbuf, sem):
    i = pl.program_id(0)
    nstep = pl.num_programs(0)

    def each_segment(tile, buf, action):
        def body(e, carry):
            idx = tile * N_EXPERTS + e

            @pl.when(len_ref[idx] > 0)
            def _():
                action(_segment_copy(stage_ref, dst_ref, len_ref, idx, sbuf.at[buf], y_hbm,
                                     sem.at[buf], False))
            return carry
        lax.fori_loop(0, N_EXPERTS, body, 0)

    @pl.when(i == 0)
    def _():
        sbuf[...] = jnp.zeros_like(sbuf)
        each_segment(0, 0, lambda copy: copy.start())

    @pl.when(i + 1 < nstep)
    def _():
        each_segment(i + 1, lax.rem(i + 1, 2), lambda copy: copy.start())

    buf = lax.rem(i, 2)
    each_segment(i, buf, lambda copy: copy.wait())

    lane = lax.broadcasted_iota(jnp.int32, (x1_ref.shape[0], sbuf.shape[1]), 1)
    weight = jnp.where(slot_ref[0] == lane, gk_ref[0], 0.0)
    for k in range(1, TOP_K):
        weight = weight + jnp.where(slot_ref[k] == lane, gk_ref[k], 0.0)
    hi = weight.astype(BF16)
    lo = (weight - hi.astype(F32)).astype(BF16)
    rows = sbuf[buf].astype(BF16)
    acc = (x1_ref[...] + jnp.dot(hi, rows, preferred_element_type=F32)
           + jnp.dot(lo, rows, preferred_element_type=F32))
    o_ref[...] = _rmsnorm(acc, gfin_ref[...])


def _gather_combine(slot_col, segs, x1, gk_col, g_fin, y):
    t = x1.shape[0]
    tm = TM_DISPATCH
    grid_spec = pltpu.PrefetchScalarGridSpec(
        num_scalar_prefetch=3,
        grid=(t // tm,),
        in_specs=[
            pl.BlockSpec((TOP_K, tm, 1), lambda i, *_: (0, i, 0)),
            pl.BlockSpec((TOP_K, tm, 1), lambda i, *_: (0, i, 0)),
            pl.BlockSpec((tm, D_MODEL), lambda i, *_: (i, 0)),
            pl.BlockSpec((1, D_MODEL), lambda i, *_: (0, 0)),
            pl.BlockSpec(memory_space=pl.ANY),
        ],
        out_specs=pl.BlockSpec((tm, D_MODEL), lambda i, *_: (i, 0)),
        scratch_shapes=[pltpu.VMEM((2, STAGE_ROWS, D_MODEL), F32), pltpu.SemaphoreType.DMA((2,))],
    )
    return pl.pallas_call(
        _gather_combine_kernel,
        grid_spec=grid_spec,
        out_shape=jax.ShapeDtypeStruct((t, D_MODEL), F32),
        compiler_params=pltpu.CompilerParams(
            dimension_semantics=("arbitrary",), vmem_limit_bytes=VMEM_LIMIT),
        name="gather_combine_norm",
    )(*segs, slot_col, gk_col, x1, g_fin, y)


def kernel(x, norm_mix, w_in, s5_lam_re, s5_lam_im, s5_b_re, s5_b_im, s5_c_re, s5_c_im, s5_d,
           s5_log_step, w_glu, w_branch_sb, w_branch_s5, w_out, norm_ffn, w_router, b_router,
           w_up, b_up, w_down, b_down, norm_final):
    bsz, seq, d = x.shape
    t = bsz * seq
    assert norm_mix.shape[0] == 1, "single-layer block"
    l = 0
    x2 = x.reshape(t, d).astype(F32)
    qkv, u, gates = _inproj(x2, norm_mix[l].reshape(1, d).astype(F32), w_in[l].astype(BF16))
    ysb = _attention(qkv.reshape(bsz, seq, 3 * SB_WIDTH))
    bblk, cblk, tab = _s5_params(s5_lam_re[l], s5_lam_im[l], s5_b_re[l], s5_b_im[l],
                                 s5_c_re[l], s5_c_im[l], s5_log_step[l])
    ys5 = _s5(u.reshape(bsz, seq, S5_WIDTH), bblk, cblk, tab,
              s5_d[l].reshape(1, S5_WIDTH).astype(F32), w_glu[l].astype(BF16))
    x1, h2, ek, pk, gk, cnt = _merge(
        ysb.reshape(t, SB_WIDTH), ys5.reshape(t, S5_WIDTH), gates, x2,
        w_branch_sb[l].astype(BF16), w_branch_s5[l].astype(BF16), w_out[l].astype(BF16),
        norm_ffn[l].reshape(1, d).astype(F32), w_router[l].T.astype(F32),
        b_router[l].reshape(N_EXPERTS, 1).astype(F32))
    slot, segs, group, tail_tile, tile_expert, tile_parity, nused, ntile = _layout(ek, pk, t)
    xg = _dispatch(slot, segs, group, tail_tile, nused, h2, ntile * TR)
    y = _experts(tile_expert, tile_parity, nused, xg, w_up[l],
                 b_up[l].reshape(N_EXPERTS, 1, 2 * D_FF), w_down[l],
                 b_down[l].reshape(N_EXPERTS, 1, D_MODEL))
    out = _gather_combine(slot.reshape(TOP_K, t, 1), segs, x1, gk.reshape(TOP_K, t, 1),
                          norm_final.reshape(1, d).astype(F32), y)
    return out.reshape(bsz, seq, d).astype(x.dtype)
```

```python
import functools

import jax
import jax.numpy as jnp
import numpy as np
from jax import lax
from jax.experimental import pallas as pl
from jax.experimental.pallas import tpu as pltpu

F32 = jnp.float32
BF16 = jnp.bfloat16

D_MODEL = 1024
SB_HEADS = 8
SB_HEAD_DIM = 64
SB_WIDTH = SB_HEADS * SB_HEAD_DIM
S5_WIDTH = 512
S5_GROUP = 16
S5_GROUPS = 32
S5_STATE = 64
N_EXPERTS = 32
TOP_K = 4
D_FF = 1024
SWIGLU_LIMIT = 7.0
SWIGLU_ALPHA = 1.702
RMS_EPS = 1e-5
OFF_U = 3 * SB_WIDTH
OFF_GATES = OFF_U + S5_WIDTH
IN_WIDTH = OFF_GATES + 2 * D_MODEL

LANES = 128
SUBLANES = 8
VMEM_LIMIT = 56 * 1024 * 1024

TM_PROJ = 512
TQ = 256
TK = 256
Q_PER_STEP = 2
EXP_UNDERFLOW = 110.0
S5_BLK = 16
TM_MERGE = 512
SUB_MERGE = 512
TR = 512
TM_DISPATCH = 256
STAGE_ROWS = TOP_K * TM_DISPATCH + N_EXPERTS * SUBLANES


def _rmsnorm(x, g):
    return x * lax.rsqrt(jnp.mean(x * x, axis=-1, keepdims=True) + RMS_EPS) * g


def _sigmoid(x):
    return 1.0 / (1.0 + jnp.exp(-x))


def _block_order(tm, inverse=False):
    row = lax.broadcasted_iota(jnp.int32, (tm, tm), 0)
    col = lax.broadcasted_iota(jnp.int32, (tm, tm), 1)
    nblk = tm // S5_BLK
    if inverse:
        return jnp.where(col == (row % S5_BLK) * nblk + row // S5_BLK, 1.0, 0.0).astype(BF16)
    return jnp.where(col == (row % nblk) * S5_BLK + row // nblk, 1.0, 0.0).astype(BF16)


def _inproj_kernel(x_ref, g_ref, w_ref, qkv_ref, u_ref, ublk_ref, gate_ref):
    hb = _rmsnorm(x_ref[...], g_ref[...]).astype(BF16)
    chunk = SB_WIDTH

    def proj(c0):
        return jnp.dot(hb, w_ref[:, c0:c0 + chunk], preferred_element_type=F32)

    qkv_ref[:, 0:chunk] = (proj(0) * (SB_HEAD_DIM ** -0.5)).astype(BF16)
    qkv_ref[:, chunk:2 * chunk] = proj(chunk).astype(BF16)
    qkv_ref[:, 2 * chunk:3 * chunk] = proj(2 * chunk).astype(BF16)
    u = proj(OFF_U).astype(BF16)
    u_ref[...] = u
    tm = u.shape[0]
    nblk = tm // S5_BLK
    by_pos = jnp.dot(_block_order(tm), u, preferred_element_type=F32).astype(BF16)
    for j in range(S5_BLK):
        ublk_ref[:, j * S5_WIDTH:(j + 1) * S5_WIDTH] = by_pos[j * nblk:(j + 1) * nblk, :]
    for c in range(2 * D_MODEL // chunk):
        gate_ref[:, c * chunk:(c + 1) * chunk] = proj(OFF_GATES + c * chunk).astype(BF16)


def _inproj(x2, g, w_bf):
    t = x2.shape[0]
    return pl.pallas_call(
        _inproj_kernel,
        grid=(t // TM_PROJ,),
        in_specs=[
            pl.BlockSpec((TM_PROJ, D_MODEL), lambda i: (i, 0)),
            pl.BlockSpec((1, D_MODEL), lambda i: (0, 0)),
            pl.BlockSpec((D_MODEL, IN_WIDTH), lambda i: (0, 0)),
        ],
        out_specs=[
            pl.BlockSpec((TM_PROJ, 3 * SB_WIDTH), lambda i: (i, 0)),
            pl.BlockSpec((TM_PROJ, S5_WIDTH), lambda i: (i, 0)),
            pl.BlockSpec((TM_PROJ // S5_BLK, S5_BLK * S5_WIDTH), lambda i: (i, 0)),
            pl.BlockSpec((TM_PROJ, 2 * D_MODEL), lambda i: (i, 0)),
        ],
        out_shape=[
            jax.ShapeDtypeStruct((t, 3 * SB_WIDTH), BF16),
            jax.ShapeDtypeStruct((t, S5_WIDTH), BF16),
            jax.ShapeDtypeStruct((t // S5_BLK, S5_BLK * S5_WIDTH), BF16),
            jax.ShapeDtypeStruct((t, 2 * D_MODEL), BF16),
        ],
        compiler_params=pltpu.CompilerParams(
            dimension_semantics=("parallel",), vmem_limit_bytes=VMEM_LIMIT),
        name="inproj",
    )(x2, g, w_bf)


def _attn_kernel(q_ref, k_ref, v_ref, o_ref, acc_ref, cb_ref):
    step = pl.program_id(2)
    nheads = LANES // SB_HEAD_DIM
    is_h0 = lax.broadcasted_iota(jnp.int32, (TQ, LANES), 1) < SB_HEAD_DIM
    row = lax.broadcasted_iota(jnp.int32, (TQ, TK), 0)
    col = lax.broadcasted_iota(jnp.int32, (TQ, TK), 1)
    tri = jnp.where(row >= col, 1.0, 0.0).astype(BF16)
    causal = col < row
    acc_ref[...] = jnp.zeros_like(acc_ref)
    cb_ref[...] = jnp.zeros_like(cb_ref)

    def scores(qm, k, mask):
        z = lax.dot_general(qm, k, (((1,), (1,)), ((), ())), preferred_element_type=F32)
        log_keep = -(jnp.maximum(z, 0.0) + jnp.log(1.0 + jnp.exp(-jnp.abs(z))))
        if mask is not None:
            log_keep = jnp.where(mask, log_keep, 0.0)
        hi = log_keep.astype(BF16)
        lo = (log_keep - hi.astype(F32)).astype(BF16)
        rcum = (jnp.dot(hi, tri, preferred_element_type=F32)
                + jnp.dot(lo, tri, preferred_element_type=F32))
        return z, rcum

    def pair(sub, qms, kt, diag):
        has_b = kt >= 1
        ka = pl.multiple_of(kt * TK, TK)
        kb = pl.multiple_of(jnp.maximum(kt - 1, 0) * TK, TK)
        k_a, v_a = k_ref[0, pl.ds(ka, TK), :], v_ref[0, pl.ds(ka, TK), :]
        k_b, v_b = k_ref[0, pl.ds(kb, TK), :], v_ref[0, pl.ds(kb, TK), :]
        mask_a = causal if diag else None
        pvs, worst = [], None
        for h, qm in enumerate(qms):
            z_a, rcum_a = scores(qm, k_a, mask_a)
            z_b, rcum_b = scores(qm, k_b, has_b)
            cb = cb_ref[nheads * sub + h]
            cb_a = cb + rcum_a[:, 0:1]
            w_a = jnp.exp(z_a + rcum_a + cb)
            if diag:
                w_a = jnp.where(causal, w_a, 0.0)
            w_b = jnp.where(has_b, jnp.exp(z_b + rcum_b + cb_a), 0.0)
            pvs.append(jnp.dot(w_a.astype(BF16), v_a, preferred_element_type=F32)
                       + jnp.dot(w_b.astype(BF16), v_b, preferred_element_type=F32))
            cb_new = cb_a + rcum_b[:, 0:1]
            cb_ref[nheads * sub + h] = cb_new
            top = jnp.max(cb_new)
            worst = top if worst is None else jnp.maximum(worst, top)
        acc_ref[sub] += jnp.where(is_h0, pvs[0], pvs[1])
        return jnp.logical_not(worst < -EXP_UNDERFLOW)

    tiles = []
    for sub in range(Q_PER_STEP):
        q = q_ref[0, sub * TQ:(sub + 1) * TQ, :]
        zero = jnp.zeros_like(q)
        qms = (jnp.where(is_h0, q, zero), jnp.where(is_h0, zero, q))
        qi = step * Q_PER_STEP + sub
        tiles.append((qms, qi, pair(sub, qms, qi, True)))

    for sub, (qms, qi, live) in enumerate(tiles):
        def cond(carry):
            kt, live = carry
            return jnp.logical_and(kt >= 0, live)

        def body(carry, sub=sub, qms=qms):
            kt, _ = carry
            return kt - 2, pair(sub, qms, kt, False)

        lax.while_loop(cond, body, (qi - 2, live))
        o_ref[0, sub * TQ:(sub + 1) * TQ, :] = acc_ref[sub].astype(BF16)


def _attention(qkv3):
    b, l, _ = qkv3.shape
    pairs = SB_WIDTH // LANES
    tq = Q_PER_STEP * TQ
    nheads = LANES // SB_HEAD_DIM
    return pl.pallas_call(
        _attn_kernel,
        grid=(b, pairs, l // tq),
        in_specs=[
            pl.BlockSpec((1, tq, LANES), lambda bi, hp, qi: (bi, qi, hp)),
            pl.BlockSpec((1, l, LANES), lambda bi, hp, qi: (bi, 0, pairs + hp)),
            pl.BlockSpec((1, l, LANES), lambda bi, hp, qi: (bi, 0, 2 * pairs + hp)),
        ],
        out_specs=pl.BlockSpec((1, tq, LANES), lambda bi, hp, qi: (bi, qi, hp)),
        out_shape=jax.ShapeDtypeStruct((b, l, SB_WIDTH), BF16),
        scratch_shapes=[pltpu.VMEM((Q_PER_STEP, TQ, LANES), F32),
                        pltpu.VMEM((Q_PER_STEP * nheads, TQ, 1), F32)],
        compiler_params=pltpu.CompilerParams(
            dimension_semantics=("parallel", "parallel", "arbitrary"),
            vmem_limit_bytes=VMEM_LIMIT),
        name="sb_attention",
    )(qkv3, qkv3, qkv3)


def _ssm_kernel(*refs, rows_per_seq):
    u_refs, (wu_ref, mintra_ref, mstate_ref, tab_ref) = refs[:S5_BLK], refs[S5_BLK:S5_BLK + 4]
    y_refs, (sre_ref, sim_ref) = refs[S5_BLK + 4:2 * S5_BLK + 4], refs[2 * S5_BLK + 4:]
    pieces = [r[...] for r in u_refs]
    width = 2 * S5_GROUP
    outs = []
    for pr in range(LANES // width):
        lanes = slice(pr * width, (pr + 1) * width)
        u = jnp.concatenate([piece[:, lanes] for piece in pieces], axis=1)
        outs.append(_ssm_pair(u, wu_ref[pr], mintra_ref[pr], mstate_ref[pr],
                              [tab_ref[pr, i] for i in range(8)], sre_ref, sim_ref, rows_per_seq))
    for j, y_ref in enumerate(y_refs):
        y_ref[...] = jnp.concatenate([o[:, j * width:(j + 1) * width] for o in outs], axis=1)


def _ssm_pair(u, wu, mintra, mstate, tabs, sre_ref, sim_ref, rows_per_seq):
    v = jnp.dot(u, wu, preferred_element_type=F32)
    sre_ref[...] = v[:, :LANES]
    sim_ref[...] = v[:, LANES:]
    nseq = u.shape[0] // rows_per_seq
    first = lax.broadcasted_iota(jnp.int32, (SUBLANES, LANES), 0) == 0

    def group(g, carry):
        out = []
        for b in range(nseq):
            rows = pl.ds(pl.multiple_of(b * rows_per_seq + g * SUBLANES, SUBLANES), SUBLANES)
            xr, xi = sre_ref[rows, :], sim_ref[rows, :]
            for stage in range(3):
                ar, ai = tabs[2 * stage], tabs[2 * stage + 1]
                sr = pltpu.roll(xr, 1 << stage, axis=0)
                si = pltpu.roll(xi, 1 << stage, axis=0)
                xr, xi = xr + (ar * sr - ai * si), xi + (ar * si + ai * sr)
            cr, ci = carry[2 * b], carry[2 * b + 1]
            xr, xi = xr + (tabs[6] * cr - tabs[7] * ci), xi + (tabs[6] * ci + tabs[7] * cr)
            sre_ref[rows, :] = jnp.where(first, cr, pltpu.roll(xr, 1, axis=0))
            sim_ref[rows, :] = jnp.where(first, ci, pltpu.roll(xi, 1, axis=0))
            out.append(jnp.broadcast_to(xr[SUBLANES - 1:, :], xr.shape))
            out.append(jnp.broadcast_to(xi[SUBLANES - 1:, :], xi.shape))
        return tuple(out)

    zero = jnp.zeros((SUBLANES, LANES), F32)
    lax.fori_loop(0, rows_per_seq // SUBLANES, group, (zero,) * (2 * nseq))
    state = jnp.concatenate([sre_ref[...], sim_ref[...]], axis=1).astype(BF16)
    return (jnp.dot(u, mintra, preferred_element_type=F32)
            + jnp.dot(state, mstate, preferred_element_type=F32)).astype(BF16)


def _ssm_params(lam_re, lam_im, b_re, b_im, c_re, c_im, log_step):
    g, p, h, blk = S5_GROUPS, S5_STATE, S5_GROUP, S5_BLK
    lr, li = lam_re.astype(F32), lam_im.astype(F32)
    step = jnp.exp(log_step.astype(F32))[:, None]
    mag = jnp.exp(lr * step)
    bar_re, bar_im = mag * jnp.cos(li * step), mag * jnp.sin(li * step)
    den = lr * lr + li * li
    f_re = ((bar_re - 1.0) * lr + bar_im * li) / den
    f_im = (bar_im * lr - (bar_re - 1.0) * li) / den
    f_re, f_im = f_re[:, :, None], f_im[:, :, None]
    bb_re = f_re * b_re.astype(F32) - f_im * b_im.astype(F32)
    bb_im = f_re * b_im.astype(F32) + f_im * b_re.astype(F32)
    cr, ci = c_re.astype(F32), c_im.astype(F32)

    def cmul(a, b):
        return a[0] * b[0] - a[1] * b[1], a[0] * b[1] + a[1] * b[0]

    def powers(base, n):
        out = [(jnp.ones_like(base[0]), jnp.zeros_like(base[0]))]
        for _ in range(n):
            out.append(cmul(out[-1], base))
        return jnp.stack([q[0] for q in out]), jnp.stack([q[1] for q in out])

    npair = g // 2
    w = 2 * blk * h
    exact = functools.partial(jnp.einsum, precision=lax.Precision.HIGHEST)
    pw_re, pw_im = powers((bar_re, bar_im), blk)
    by_pair = lambda m: m.reshape((npair, 2) + m.shape[1:])
    bt_re, bt_im = (by_pair(m.transpose(0, 2, 1)) for m in (bb_re, bb_im))
    ct_re, ct_im = (by_pair(m.transpose(0, 2, 1)) for m in (cr, ci))
    pk_re, pk_im = (by_pair(m.transpose(1, 2, 0)) for m in (pw_re, pw_im))
    same = np.eye(2, dtype=np.float32)

    sel_state = np.einsum('ab,pr->apbr', same, np.eye(p, dtype=np.float32))
    sel_state = np.stack([sel_state, sel_state], axis=2).reshape(2, p, 4 * p)
    part0 = np.concatenate([np.eye(p), np.eye(p), np.zeros((p, 2 * p))], axis=1).astype(np.float32)
    part1 = np.concatenate([np.zeros((p, 2 * p)), np.eye(p), np.eye(p)], axis=1).astype(np.float32)
    rv_re, rv_im = (by_pair(m[:blk][::-1].transpose(1, 0, 2)).transpose(0, 2, 1, 3)
                    for m in (pw_re, pw_im))
    rx_re, rx_im = (exact('niap,apC->niaC', m, sel_state)[:, :, :, None, :] for m in (rv_re, rv_im))
    bx = (exact('nacp,pC->nacC', bt_re, part0) + exact('nacp,pC->nacC', bt_im, part1))[:, None]
    by = (exact('nacp,pC->nacC', bt_re, part1) - exact('nacp,pC->nacC', bt_im, part0))[:, None]
    wu_pair = (rx_re * bx + rx_im * by).reshape(npair, w, 4 * p)

    sel_out = np.einsum('ab,hr->ahbr', same, np.eye(h, dtype=np.float32))
    sel_out = np.tile(sel_out.reshape(2, h, 1, 2 * h), (1, 1, blk, 1)).reshape(2, h, w)
    sel_pos = np.repeat(np.eye(blk + 1, dtype=np.float32)[:, 1:], 2 * h, axis=1)
    cx_re, cx_im = (exact('naph,ahC->napC', m, sel_out) for m in (ct_re, ct_im))
    px_re, px_im = (exact('napk,kC->napC', m, sel_pos) for m in (pk_re, pk_im))
    mstate = jnp.stack([cx_re * px_re - cx_im * px_im, -(cx_re * px_im + cx_im * px_re)], axis=1)
    mstate = mstate.reshape(npair, 4 * p, w)

    cl_re = ct_re[:, :, :, None, :] * pk_re[..., :blk, None] - ct_im[:, :, :, None, :] * pk_im[..., :blk, None]
    cl_im = ct_re[:, :, :, None, :] * pk_im[..., :blk, None] + ct_im[:, :, :, None, :] * pk_re[..., :blk, None]
    kern = exact('napkh,nacp->nackh', cl_re, bt_re) - exact('napkh,nacp->nackh', cl_im, bt_im)
    sel_lag = np.einsum('ab,kl,hr->akhlbr', same, np.eye(blk, dtype=np.float32),
                        np.eye(h, dtype=np.float32)).reshape(2, blk, h, w)
    slab = exact('nackh,akhC->nacC', kern, sel_lag).reshape(npair, 2 * h, w)
    step_w = 2 * h
    mintra = jnp.stack([jnp.pad(slab[:, :, :w - i * step_w], ((0, 0), (0, 0), (i * step_w, 0)))
                        for i in range(blk)], axis=1).reshape(npair, w, w)

    big = (pw_re[blk].reshape(npair, 1, 2 * p), pw_im[blk].reshape(npair, 1, 2 * p))
    lp_re, lp_im = powers(big, SUBLANES)
    sub = jnp.arange(SUBLANES)[None, :, None]
    rows = []
    for shift in (1, 2, 4):
        for part in (lp_re[shift], lp_im[shift]):
            rows.append(jnp.where(sub >= shift, part, 0.0))
    rows.append(jnp.concatenate(list(lp_re[1:]), axis=1))
    rows.append(jnp.concatenate(list(lp_im[1:]), axis=1))
    tab = jnp.stack(rows, axis=1)
    return wu_pair.astype(BF16), mintra.astype(BF16), mstate.astype(BF16), tab


def _ssm(uflat, seq, wu, mintra, mstate, tab):
    nrow = uflat.shape[0]
    ntile = S5_WIDTH // LANES
    ppt = wu.shape[0] // ntile
    piece = lambda j: pl.BlockSpec((nrow, LANES), lambda q: (0, j * ntile + q))
    per_tile = lambda a: pl.BlockSpec((ppt,) + a.shape[1:], lambda q: (q,) + (0,) * (a.ndim - 1))
    return pl.pallas_call(
        functools.partial(_ssm_kernel, rows_per_seq=seq // S5_BLK),
        grid=(ntile,),
        in_specs=([piece(j) for j in range(S5_BLK)]
                  + [per_tile(wu), per_tile(mintra), per_tile(mstate), per_tile(tab)]),
        out_specs=[pl.BlockSpec((nrow, LANES), lambda q: (0, q))] * S5_BLK,
        out_shape=[jax.ShapeDtypeStruct((nrow, S5_WIDTH), BF16)] * S5_BLK,
        scratch_shapes=[pltpu.VMEM((nrow, LANES), F32), pltpu.VMEM((nrow, LANES), F32)],
        compiler_params=pltpu.CompilerParams(
            dimension_semantics=("parallel",), vmem_limit_bytes=VMEM_LIMIT),
        name="s5_blocked",
    )(*([uflat] * S5_BLK), wu, mintra, mstate, tab)


def _merge_kernel(*refs):
    ssm_refs = refs[:S5_BLK]
    (ysb_ref, u_ref, gate_ref, x_ref, d_ref, wglu_ref, wsb_ref, ws5_ref, wout_ref, g_ref, wr_ref,
     br_ref, x1_ref, h2_ref, ek_ref, pk_ref, gk_ref, cnt_ref, carry_ref) = refs[S5_BLK:]

    @pl.when(pl.program_id(0) == 0)
    def _():
        carry_ref[...] = jnp.zeros_like(carry_ref)

    sub = SUB_MERGE
    trow = lax.broadcasted_iota(jnp.int32, (sub, sub), 0)
    tcol = lax.broadcasted_iota(jnp.int32, (sub, sub), 1)
    before = jnp.where(trow < tcol, 1.0, 0.0).astype(BF16)
    erow = lax.broadcasted_iota(jnp.int32, (N_EXPERTS, N_EXPERTS), 0)
    ecol = lax.broadcasted_iota(jnp.int32, (N_EXPERTS, N_EXPERTS), 1)
    lower = jnp.where(ecol < erow, 1.0, 0.0).astype(BF16)
    eidx = lax.broadcasted_iota(jnp.int32, (N_EXPERTS, sub), 0)
    eidx_f = eidx.astype(F32)

    for s in range(x_ref.shape[0] // sub):
        rows = slice(s * sub, (s + 1) * sub)
        nblk = sub // S5_BLK
        by_pos = jnp.concatenate([r[s * nblk:(s + 1) * nblk, :] for r in ssm_refs], axis=0)
        ssm = jnp.dot(_block_order(sub, inverse=True), by_pos, preferred_element_type=F32)
        y5 = jax.nn.gelu(ssm + d_ref[...] * u_ref[rows, :].astype(F32))
        glu = jnp.dot(y5.astype(BF16), wglu_ref[...], preferred_element_type=F32)
        ys5 = (y5 * _sigmoid(glu)).astype(BF16)
        a = jnp.dot(ysb_ref[rows, :], wsb_ref[...], preferred_element_type=F32)
        b = jnp.dot(ys5, ws5_ref[...], preferred_element_type=F32)
        ga = _sigmoid(gate_ref[rows, :D_MODEL].astype(F32))
        gb = _sigmoid(gate_ref[rows, D_MODEL:].astype(F32))
        merged = (ga * a + gb * b).astype(BF16)
        x1 = x_ref[rows, :] + jnp.dot(merged, wout_ref[...], preferred_element_type=F32)
        x1_ref[rows, :] = x1
        h2 = _rmsnorm(x1, g_ref[...])
        h2_ref[rows, :] = h2

        logits = lax.dot_general(wr_ref[...], h2, (((1,), (1,)), ((), ())),
                                 precision=lax.Precision.HIGHEST,
                                 preferred_element_type=F32) + br_ref[...]
        work = logits
        sel = jnp.zeros(logits.shape, jnp.bool_)
        top = None
        for k in range(TOP_K):
            m = jnp.max(work, axis=0, keepdims=True)
            first = jnp.min(jnp.where(work == m, eidx, N_EXPERTS), axis=0, keepdims=True)
            pick = eidx == first
            sel = jnp.logical_or(sel, pick)
            work = jnp.where(pick, -jnp.inf, work)
            if k == 0:
                top = m
        e = jnp.where(sel, jnp.exp(logits - top), 0.0)
        gate = e / jnp.sum(e, axis=0, keepdims=True)

        mask = jnp.where(sel, 1.0, 0.0)
        pos = jnp.dot(mask.astype(BF16), before, preferred_element_type=F32) + carry_ref[...]
        carry_ref[...] += jnp.sum(mask, axis=1, keepdims=True)

        rank = jnp.dot(lower, mask.astype(BF16), preferred_element_type=F32)
        for k in range(TOP_K):
            ind = jnp.logical_and(sel, rank == float(k))
            pick1 = lambda v: jnp.sum(jnp.where(ind, v, 0.0), axis=0, keepdims=True)
            ek_ref[k:k + 1, rows] = pick1(eidx_f).astype(jnp.int32)
            pk_ref[k:k + 1, rows] = pick1(pos).astype(jnp.int32)
            gk_ref[k:k + 1, rows] = pick1(gate)

    cnt_ref[...] = jnp.broadcast_to(carry_ref[...], cnt_ref.shape)


def _merge(ssm_pieces, ysb, u, gates, x2, d_row, wglu, wsb, ws5, wout, g_ffn, wr_t, br_col):
    t = x2.shape[0]
    tm = TM_MERGE
    const = lambda *shape: pl.BlockSpec(shape, lambda i: (0,) * len(shape))
    return pl.pallas_call(
        _merge_kernel,
        grid=(t // tm,),
        in_specs=[pl.BlockSpec((tm // S5_BLK, S5_WIDTH), lambda i: (i, 0))] * S5_BLK + [
            pl.BlockSpec((tm, SB_WIDTH), lambda i: (i, 0)),
            pl.BlockSpec((tm, S5_WIDTH), lambda i: (i, 0)),
            pl.BlockSpec((tm, 2 * D_MODEL), lambda i: (i, 0)),
            pl.BlockSpec((tm, D_MODEL), lambda i: (i, 0)),
            const(1, S5_WIDTH), const(S5_WIDTH, S5_WIDTH),
            const(SB_WIDTH, D_MODEL), const(S5_WIDTH, D_MODEL), const(D_MODEL, D_MODEL),
            const(1, D_MODEL), const(N_EXPERTS, D_MODEL), const(N_EXPERTS, 1),
        ],
        out_specs=[
            pl.BlockSpec((tm, D_MODEL), lambda i: (i, 0)),
            pl.BlockSpec((tm, D_MODEL), lambda i: (i, 0)),
            pl.BlockSpec((TOP_K, tm), lambda i: (0, i)),
            pl.BlockSpec((TOP_K, tm), lambda i: (0, i)),
            pl.BlockSpec((TOP_K, tm), lambda i: (0, i)),
            pl.BlockSpec((N_EXPERTS, LANES), lambda i: (0, 0)),
        ],
        out_shape=[
            jax.ShapeDtypeStruct((t, D_MODEL), F32),
            jax.ShapeDtypeStruct((t, D_MODEL), F32),
            jax.ShapeDtypeStruct((TOP_K, t), jnp.int32),
            jax.ShapeDtypeStruct((TOP_K, t), jnp.int32),
            jax.ShapeDtypeStruct((TOP_K, t), F32),
            jax.ShapeDtypeStruct((N_EXPERTS, LANES), F32),
        ],
        scratch_shapes=[pltpu.VMEM((N_EXPERTS, 1), F32)],
        compiler_params=pltpu.CompilerParams(
            dimension_semantics=("arbitrary",), vmem_limit_bytes=VMEM_LIMIT),
        name="merge_router",
    )(*ssm_pieces, ysb, u, gates, x2, d_row, wglu, wsb, ws5, wout, g_ffn, wr_t, br_col)


def _layout(ek, pk, t):
    tm = TM_DISPATCH
    ni = t // tm
    ek3, pk3 = ek.reshape(TOP_K, ni, tm), pk.reshape(TOP_K, ni, tm)
    cnt = jnp.stack([jnp.sum(ek3 == e, axis=(0, 2)) for e in range(N_EXPERTS)], axis=1)
    cnt = cnt.astype(jnp.int32)
    seg = (cnt + SUBLANES - 1) // SUBLANES * SUBLANES
    group = jnp.sum(seg, axis=0)
    padded = (group + TR - 1) // TR * TR
    ends = jnp.cumsum(padded)
    dst = (ends - padded)[None, :] + jnp.cumsum(seg, axis=0) - seg
    stage = jnp.cumsum(seg, axis=1) - seg
    base = stage - (jnp.cumsum(cnt, axis=0) - cnt)
    slot = pk3
    for e in range(N_EXPERTS):
        slot = slot + jnp.where(ek3 == e, base[None, :, e, None], 0)
    ntile = (TOP_K * t + ni * N_EXPERTS * (SUBLANES - 1)) // TR + 1 + N_EXPERTS
    nused = ends[-1] // TR
    tile_start = jnp.arange(ntile, dtype=jnp.int32) * TR
    tile_expert = jnp.sum(tile_start[:, None] >= ends[None, :], axis=1).astype(jnp.int32)
    tile_expert = jnp.minimum(tile_expert, jnp.take(tile_expert, nused - 1))
    changes = jnp.concatenate([jnp.zeros((1,), jnp.int32),
                               (tile_expert[1:] != tile_expert[:-1]).astype(jnp.int32)])
    tile_parity = jnp.cumsum(changes) % 2
    tail_tile = jnp.maximum(ends // TR - 1, 0).astype(jnp.int32)
    segs = (stage.reshape(-1), dst.reshape(-1).astype(jnp.int32), seg.reshape(-1))
    return (slot.reshape(TOP_K, t), segs, group.astype(jnp.int32), tail_tile, tile_expert,
            tile_parity.astype(jnp.int32), nused.reshape(1).astype(jnp.int32), ntile)


def _segment_copy(stage_ref, dst_ref, len_ref, idx, staging, buffer, sem, to_buffer):
    aligned = lambda v: pl.multiple_of(v, SUBLANES)
    n = aligned(len_ref[idx])
    a = staging.at[pl.ds(aligned(stage_ref[idx]), n), :]
    b = buffer.at[pl.ds(aligned(dst_ref[idx]), n), :]
    return pltpu.make_async_copy(a, b, sem) if to_buffer else pltpu.make_async_copy(b, a, sem)


def _dispatch_kernel(stage_ref, dst_ref, len_ref, grp_ref, tail_ref, nused_ref, slot_ref, h2_ref,
                     xg_hbm, s_ref, zero_ref, sem, zsem):
    i = pl.program_id(0)

    @pl.when(i == 0)
    def _():
        zero_ref[...] = jnp.zeros_like(zero_ref)
        fill = lambda e: pltpu.make_async_copy(
            zero_ref, xg_hbm.at[pl.ds(pl.multiple_of(tail_ref[e] * TR, TR), TR), :], zsem)
        for e in range(N_EXPERTS):
            @pl.when(grp_ref[e] > 0)
            def _():
                fill(e).start()
        for e in range(N_EXPERTS):
            @pl.when(grp_ref[e] > 0)
            def _():
                fill(e).wait()

        spare = lambda r: pltpu.make_async_copy(
            zero_ref, xg_hbm.at[pl.ds(pl.multiple_of(r * TR, TR), TR), :], zsem)
        ntile = xg_hbm.shape[0] // TR
        lax.fori_loop(nused_ref[0], ntile, lambda r, c: (spare(r).start(), c)[1], 0)
        lax.fori_loop(nused_ref[0], ntile, lambda r, c: (spare(r).wait(), c)[1], 0)

    rowid = lax.broadcasted_iota(jnp.int32, (s_ref.shape[1], h2_ref.shape[0]), 0)
    hit = slot_ref[0:1, :] == rowid
    for k in range(1, TOP_K):
        hit = jnp.logical_or(hit, slot_ref[k:k + 1, :] == rowid)
    onehot = jnp.where(hit, 1.0, 0.0).astype(BF16)
    buf = lax.rem(i, 2)
    s_ref[buf] = jnp.dot(onehot, h2_ref[...].astype(BF16), preferred_element_type=F32)

    def each_segment(tile, buf, action):
        def body(e, carry):
            idx = tile * N_EXPERTS + e

            @pl.when(len_ref[idx] > 0)
            def _():
                action(_segment_copy(stage_ref, dst_ref, len_ref, idx, s_ref.at[buf], xg_hbm,
                                     sem.at[buf], True))
            return carry
        lax.fori_loop(0, N_EXPERTS, body, 0)

    each_segment(i, buf, lambda copy: copy.start())

    @pl.when(i >= 1)
    def _():
        each_segment(i - 1, 1 - buf, lambda copy: copy.wait())

    @pl.when(i == pl.num_programs(0) - 1)
    def _():
        each_segment(i, buf, lambda copy: copy.wait())


def _dispatch(slot, segs, group, tail_tile, nused, h2, nrow):
    t = h2.shape[0]
    tm = TM_DISPATCH
    grid_spec = pltpu.PrefetchScalarGridSpec(
        num_scalar_prefetch=6,
        grid=(t // tm,),
        in_specs=[pl.BlockSpec((TOP_K, tm), lambda i, *_: (0, i)),
                  pl.BlockSpec((tm, D_MODEL), lambda i, *_: (i, 0))],
        out_specs=pl.BlockSpec(memory_space=pl.ANY),
        scratch_shapes=[pltpu.VMEM((2, STAGE_ROWS, D_MODEL), F32), pltpu.VMEM((TR, D_MODEL), F32),
                        pltpu.SemaphoreType.DMA((2,)), pltpu.SemaphoreType.DMA(())],
    )
    return pl.pallas_call(
        _dispatch_kernel,
        grid_spec=grid_spec,
        out_shape=jax.ShapeDtypeStruct((nrow, D_MODEL), F32),
        compiler_params=pltpu.CompilerParams(
            dimension_semantics=("arbitrary",), vmem_limit_bytes=VMEM_LIMIT),
        name="moe_dispatch",
    )(*segs, group, tail_tile, nused, slot, h2)


def _experts_kernel(te_ref, par_ref, nused_ref, xg_ref, wup_ref, bup_ref, wdn_ref, bdn_ref, y_ref,
                    wup_bf, wdn_bf):
    s = pl.program_id(0)
    nused = nused_ref[0]
    r = s - 1

    @pl.when(jnp.logical_and(r >= 0, r < nused))
    def _():
        slot = par_ref[jnp.maximum(r, 0)]
        x = xg_ref[...].astype(BF16)
        hid = jnp.dot(x, wup_bf[slot], preferred_element_type=F32) + bup_ref[0]
        glu = jnp.minimum(hid[:, :D_FF], SWIGLU_LIMIT)
        lin = jnp.clip(hid[:, D_FF:], -SWIGLU_LIMIT, SWIGLU_LIMIT)
        act = glu * _sigmoid(SWIGLU_ALPHA * glu) * (lin + 1.0)
        y_ref[...] = jnp.dot(act.astype(BF16), wdn_bf[slot], preferred_element_type=F32) + bdn_ref[0]

    @pl.when(r >= nused)
    def _():
        y_ref[...] = jnp.zeros_like(y_ref)

    last = te_ref.shape[0] - 1
    cur = jnp.minimum(s, last)
    new_expert = jnp.logical_or(s == 0, te_ref[cur] != te_ref[jnp.maximum(cur - 1, 0)])

    @pl.when(jnp.logical_and(s < nused, new_expert))
    def _():
        slot = par_ref[cur]
        wup_bf[slot] = wup_ref[0].astype(BF16)
        wdn_bf[slot] = wdn_ref[0].astype(BF16)


def _experts(tile_expert, tile_parity, nused, xg, w_up, b_up, w_down, b_down):
    ntile = tile_expert.shape[0]
    ahead = lambda s, te, par, nu: (te[jnp.minimum(s, ntile - 1)], 0, 0)
    current = lambda s, te, par, nu: (te[jnp.maximum(s - 1, 0)], 0, 0)
    by_block = lambda s, te, par, nu: (jnp.maximum(s - 1, 0), 0)
    grid_spec = pltpu.PrefetchScalarGridSpec(
        num_scalar_prefetch=3,
        grid=(ntile + 1,),
        in_specs=[
            pl.BlockSpec((TR, D_MODEL), by_block),
            pl.BlockSpec((1, D_MODEL, 2 * D_FF), ahead),
            pl.BlockSpec((1, 1, 2 * D_FF), current),
            pl.BlockSpec((1, D_FF, D_MODEL), ahead),
            pl.BlockSpec((1, 1, D_MODEL), current),
        ],
        out_specs=pl.BlockSpec((TR, D_MODEL), by_block),
        scratch_shapes=[pltpu.VMEM((2, D_MODEL, 2 * D_FF), BF16),
                        pltpu.VMEM((2, D_FF, D_MODEL), BF16)],
    )
    return pl.pallas_call(
        _experts_kernel,
        grid_spec=grid_spec,
        out_shape=jax.ShapeDtypeStruct(xg.shape, F32),
        compiler_params=pltpu.CompilerParams(
            dimension_semantics=("arbitrary",), vmem_limit_bytes=VMEM_LIMIT),
        name="moe_experts",
    )(tile_expert, tile_parity, nused, xg, w_up, b_up, w_down, b_down)


def _gather_combine_kernel(stage_ref, dst_ref, len_ref, slot_ref, gk_ref, x1_ref, gfin_ref, y_hbm,
                           o_ref, sbuf, sem):
    i = pl.program_id(0)
    nstep = pl.num_programs(0)

    def each_segment(tile, buf, action):
        def body(e, carry):
            idx = tile * N_EXPERTS + e

            @pl.when(len_ref[idx] > 0)
            def _():
                action(_segment_copy(stage_ref, dst_ref, len_ref, idx, sbuf.at[buf], y_hbm,
                                     sem.at[buf], False))
            return carry
        lax.fori_loop(0, N_EXPERTS, body, 0)

    @pl.when(i == 0)
    def _():
        sbuf[...] = jnp.zeros_like(sbuf)
        each_segment(0, 0, lambda copy: copy.start())

    @pl.when(i + 1 < nstep)
    def _():
        each_segment(i + 1, lax.rem(i + 1, 2), lambda copy: copy.start())

    buf = lax.rem(i, 2)
    each_segment(i, buf, lambda copy: copy.wait())

    lane = lax.broadcasted_iota(jnp.int32, (x1_ref.shape[0], sbuf.shape[1]), 1)
    weight = jnp.where(slot_ref[0] == lane, gk_ref[0], 0.0)
    for k in range(1, TOP_K):
        weight = weight + jnp.where(slot_ref[k] == lane, gk_ref[k], 0.0)
    hi = weight.astype(BF16)
    lo = (weight - hi.astype(F32)).astype(BF16)
    rows = sbuf[buf].astype(BF16)
    acc = (x1_ref[...] + jnp.dot(hi, rows, preferred_element_type=F32)
           + jnp.dot(lo, rows, preferred_element_type=F32))
    o_ref[...] = _rmsnorm(acc, gfin_ref[...])


def _gather_combine(slot_col, segs, x1, gk_col, g_fin, y):
    t = x1.shape[0]
    tm = TM_DISPATCH
    grid_spec = pltpu.PrefetchScalarGridSpec(
        num_scalar_prefetch=3,
        grid=(t // tm,),
        in_specs=[
            pl.BlockSpec((TOP_K, tm, 1), lambda i, *_: (0, i, 0)),
            pl.BlockSpec((TOP_K, tm, 1), lambda i, *_: (0, i, 0)),
            pl.BlockSpec((tm, D_MODEL), lambda i, *_: (i, 0)),
            pl.BlockSpec((1, D_MODEL), lambda i, *_: (0, 0)),
            pl.BlockSpec(memory_space=pl.ANY),
        ],
        out_specs=pl.BlockSpec((tm, D_MODEL), lambda i, *_: (i, 0)),
        scratch_shapes=[pltpu.VMEM((2, STAGE_ROWS, D_MODEL), F32), pltpu.SemaphoreType.DMA((2,))],
    )
    return pl.pallas_call(
        _gather_combine_kernel,
        grid_spec=grid_spec,
        out_shape=jax.ShapeDtypeStruct((t, D_MODEL), F32),
        compiler_params=pltpu.CompilerParams(
            dimension_semantics=("arbitrary",), vmem_limit_bytes=VMEM_LIMIT),
        name="gather_combine_norm",
    )(*segs, slot_col, gk_col, x1, g_fin, y)


def kernel(x, norm_mix, w_in, s5_lam_re, s5_lam_im, s5_b_re, s5_b_im, s5_c_re, s5_c_im, s5_d,
           s5_log_step, w_glu, w_branch_sb, w_branch_s5, w_out, norm_ffn, w_router, b_router,
           w_up, b_up, w_down, b_down, norm_final):
    bsz, seq, d = x.shape
    t = bsz * seq
    assert norm_mix.shape[0] == 1, "single-layer block"
    l = 0
    x2 = x.reshape(t, d).astype(F32)
    qkv, u, ublk, gates = _inproj(x2, norm_mix[l].reshape(1, d).astype(F32), w_in[l].astype(BF16))
    ysb = _attention(qkv.reshape(bsz, seq, 3 * SB_WIDTH))
    ssm = _ssm(ublk, seq, *_ssm_params(s5_lam_re[l], s5_lam_im[l], s5_b_re[l], s5_b_im[l],
                                       s5_c_re[l], s5_c_im[l], s5_log_step[l]))
    x1, h2, ek, pk, gk, cnt = _merge(
        ssm, ysb.reshape(t, SB_WIDTH), u, gates, x2,
        s5_d[l].reshape(1, S5_WIDTH).astype(F32), w_glu[l].astype(BF16),
        w_branch_sb[l].astype(BF16), w_branch_s5[l].astype(BF16), w_out[l].astype(BF16),
        norm_ffn[l].reshape(1, d).astype(F32), w_router[l].T.astype(F32),
        b_router[l].reshape(N_EXPERTS, 1).astype(F32))
    slot, segs, group, tail_tile, tile_expert, tile_parity, nused, ntile = _layout(ek, pk, t)
    xg = _dispatch(slot, segs, group, tail_tile, nused, h2, ntile * TR)
    y = _experts(tile_expert, tile_parity, nused, xg, w_up[l],
                 b_up[l].reshape(N_EXPERTS, 1, 2 * D_FF), w_down[l],
                 b_down[l].reshape(N_EXPERTS, 1, D_MODEL))
    out = _gather_combine(slot.reshape(TOP_K, t, 1), segs, x1, gk.reshape(TOP_K, t, 1),
                          norm_final.reshape(1, d).astype(F32), y)
    return out.reshape(bsz, seq, d).astype(x.dtype)
```

```python
import functools

import jax
import jax.numpy as jnp
import numpy as np
from jax import lax
from jax.experimental import pallas as pl
from jax.experimental.pallas import tpu as pltpu

F32 = jnp.float32
BF16 = jnp.bfloat16

D_MODEL = 1024
SB_HEADS = 8
SB_HEAD_DIM = 64
SB_WIDTH = SB_HEADS * SB_HEAD_DIM
S5_WIDTH = 512
S5_GROUP = 16
S5_GROUPS = 32
S5_STATE = 64
N_EXPERTS = 32
TOP_K = 4
D_FF = 1024
SWIGLU_LIMIT = 7.0
SWIGLU_ALPHA = 1.702
RMS_EPS = 1e-5
OFF_U = 3 * SB_WIDTH
OFF_GATES = OFF_U + S5_WIDTH
IN_WIDTH = OFF_GATES + 2 * D_MODEL

LANES = 128
SUBLANES = 8
VMEM_LIMIT = 56 * 1024 * 1024

TM_PROJ = 512
TQ = 256
TK = 256
Q_PER_STEP = 2
EXP_UNDERFLOW = 110.0
S5_BLK = 16
TM_MERGE = 512
SUB_MERGE = 512
TR = 512
TM_DISPATCH = 256
STAGE_ROWS = TOP_K * TM_DISPATCH + N_EXPERTS * SUBLANES


def _rmsnorm(x, g):
    return x * lax.rsqrt(jnp.mean(x * x, axis=-1, keepdims=True) + RMS_EPS) * g


def _sigmoid(x):
    return 1.0 / (1.0 + jnp.exp(-x))


def _block_order(tm, inverse=False):
    row = lax.broadcasted_iota(jnp.int32, (tm, tm), 0)
    col = lax.broadcasted_iota(jnp.int32, (tm, tm), 1)
    nblk = tm // S5_BLK
    if inverse:
        return jnp.where(col == (row % S5_BLK) * nblk + row // S5_BLK, 1.0, 0.0).astype(BF16)
    return jnp.where(col == (row % nblk) * S5_BLK + row // nblk, 1.0, 0.0).astype(BF16)


def _inproj_kernel(x_ref, g_ref, w_ref, qkv_ref, u_ref, ublk_ref, gate_ref):
    hb = _rmsnorm(x_ref[...], g_ref[...]).astype(BF16)
    chunk = SB_WIDTH

    def proj(c0):
        return jnp.dot(hb, w_ref[:, c0:c0 + chunk], preferred_element_type=F32)

    qkv_ref[:, 0:chunk] = (proj(0) * (SB_HEAD_DIM ** -0.5)).astype(BF16)
    qkv_ref[:, chunk:2 * chunk] = proj(chunk).astype(BF16)
    qkv_ref[:, 2 * chunk:3 * chunk] = proj(2 * chunk).astype(BF16)
    u = proj(OFF_U).astype(BF16)
    u_ref[...] = u
    tm = u.shape[0]
    nblk = tm // S5_BLK
    by_pos = jnp.dot(_block_order(tm), u, preferred_element_type=F32).astype(BF16)
    for j in range(S5_BLK):
        ublk_ref[:, j * S5_WIDTH:(j + 1) * S5_WIDTH] = by_pos[j * nblk:(j + 1) * nblk, :]
    for c in range(2 * D_MODEL // chunk):
        gate_ref[:, c * chunk:(c + 1) * chunk] = proj(OFF_GATES + c * chunk).astype(BF16)


def _inproj(x2, g, w_bf):
    t = x2.shape[0]
    return pl.pallas_call(
        _inproj_kernel,
        grid=(t // TM_PROJ,),
        in_specs=[
            pl.BlockSpec((TM_PROJ, D_MODEL), lambda i: (i, 0)),
            pl.BlockSpec((1, D_MODEL), lambda i: (0, 0)),
            pl.BlockSpec((D_MODEL, IN_WIDTH), lambda i: (0, 0)),
        ],
        out_specs=[
            pl.BlockSpec((TM_PROJ, 3 * SB_WIDTH), lambda i: (i, 0)),
            pl.BlockSpec((TM_PROJ, S5_WIDTH), lambda i: (i, 0)),
            pl.BlockSpec((TM_PROJ // S5_BLK, S5_BLK * S5_WIDTH), lambda i: (i, 0)),
            pl.BlockSpec((TM_PROJ, 2 * D_MODEL), lambda i: (i, 0)),
        ],
        out_shape=[
            jax.ShapeDtypeStruct((t, 3 * SB_WIDTH), BF16),
            jax.ShapeDtypeStruct((t, S5_WIDTH), BF16),
            jax.ShapeDtypeStruct((t // S5_BLK, S5_BLK * S5_WIDTH), BF16),
            jax.ShapeDtypeStruct((t, 2 * D_MODEL), BF16),
        ],
        compiler_params=pltpu.CompilerParams(
            dimension_semantics=("parallel",), vmem_limit_bytes=VMEM_LIMIT),
        name="inproj",
    )(x2, g, w_bf)


def _attn_kernel(q_ref, k_ref, v_ref, o_ref, acc_ref, cb_ref):
    step = pl.program_id(2)
    nheads = LANES // SB_HEAD_DIM
    is_h0 = lax.broadcasted_iota(jnp.int32, (TQ, LANES), 1) < SB_HEAD_DIM
    row = lax.broadcasted_iota(jnp.int32, (TQ, TK), 0)
    col = lax.broadcasted_iota(jnp.int32, (TQ, TK), 1)
    tri = jnp.where(row >= col, 1.0, 0.0).astype(BF16)
    causal = col < row
    acc_ref[...] = jnp.zeros_like(acc_ref)
    cb_ref[...] = jnp.zeros_like(cb_ref)

    def scores(qm, k, mask):
        z = lax.dot_general(qm, k, (((1,), (1,)), ((), ())), preferred_element_type=F32)
        log_keep = -(jnp.maximum(z, 0.0) + jnp.log(1.0 + jnp.exp(-jnp.abs(z))))
        if mask is not None:
            log_keep = jnp.where(mask, log_keep, 0.0)
        hi = log_keep.astype(BF16)
        lo = (log_keep - hi.astype(F32)).astype(BF16)
        rcum = (jnp.dot(hi, tri, preferred_element_type=F32)
                + jnp.dot(lo, tri, preferred_element_type=F32))
        return z, rcum

    def pair(sub, qms, kt, diag):
        has_b = kt >= 1
        ka = pl.multiple_of(kt * TK, TK)
        kb = pl.multiple_of(jnp.maximum(kt - 1, 0) * TK, TK)
        k_a, v_a = k_ref[0, pl.ds(ka, TK), :], v_ref[0, pl.ds(ka, TK), :]
        k_b, v_b = k_ref[0, pl.ds(kb, TK), :], v_ref[0, pl.ds(kb, TK), :]
        mask_a = causal if diag else None
        pvs, worst = [], None
        for h, qm in enumerate(qms):
            z_a, rcum_a = scores(qm, k_a, mask_a)
            z_b, rcum_b = scores(qm, k_b, has_b)
            cb = cb_ref[nheads * sub + h]
            cb_a = cb + rcum_a[:, 0:1]
            w_a = jnp.exp(z_a + rcum_a + cb)
            if diag:
                w_a = jnp.where(causal, w_a, 0.0)
            w_b = jnp.where(has_b, jnp.exp(z_b + rcum_b + cb_a), 0.0)
            pvs.append(jnp.dot(w_a.astype(BF16), v_a, preferred_element_type=F32)
                       + jnp.dot(w_b.astype(BF16), v_b, preferred_element_type=F32))
            cb_new = cb_a + rcum_b[:, 0:1]
            cb_ref[nheads * sub + h] = cb_new
            top = jnp.max(cb_new)
            worst = top if worst is None else jnp.maximum(worst, top)
        acc_ref[sub] += jnp.where(is_h0, pvs[0], pvs[1])
        return jnp.logical_not(worst < -EXP_UNDERFLOW)

    tiles = []
    for sub in range(Q_PER_STEP):
        q = q_ref[0, sub * TQ:(sub + 1) * TQ, :]
        zero = jnp.zeros_like(q)
        qms = (jnp.where(is_h0, q, zero), jnp.where(is_h0, zero, q))
        qi = step * Q_PER_STEP + sub
        tiles.append((qms, qi, pair(sub, qms, qi, True)))

    for sub, (qms, qi, live) in enumerate(tiles):
        def cond(carry):
            kt, live = carry
            return jnp.logical_and(kt >= 0, live)

        def body(carry, sub=sub, qms=qms):
            kt, _ = carry
            return kt - 2, pair(sub, qms, kt, False)

        lax.while_loop(cond, body, (qi - 2, live))
        o_ref[0, sub * TQ:(sub + 1) * TQ, :] = acc_ref[sub].astype(BF16)


def _attention(qkv3):
    b, l, _ = qkv3.shape
    pairs = SB_WIDTH // LANES
    tq = Q_PER_STEP * TQ
    nheads = LANES // SB_HEAD_DIM
    return pl.pallas_call(
        _attn_kernel,
        grid=(b, pairs, l // tq),
        in_specs=[
            pl.BlockSpec((1, tq, LANES), lambda bi, hp, qi: (bi, qi, hp)),
            pl.BlockSpec((1, l, LANES), lambda bi, hp, qi: (bi, 0, pairs + hp)),
            pl.BlockSpec((1, l, LANES), lambda bi, hp, qi: (bi, 0, 2 * pairs + hp)),
        ],
        out_specs=pl.BlockSpec((1, tq, LANES), lambda bi, hp, qi: (bi, qi, hp)),
        out_shape=jax.ShapeDtypeStruct((b, l, SB_WIDTH), BF16),
        scratch_shapes=[pltpu.VMEM((Q_PER_STEP, TQ, LANES), F32),
                        pltpu.VMEM((Q_PER_STEP * nheads, TQ, 1), F32)],
        compiler_params=pltpu.CompilerParams(
            dimension_semantics=("parallel", "parallel", "arbitrary"),
            vmem_limit_bytes=VMEM_LIMIT),
        name="sb_attention",
    )(qkv3, qkv3, qkv3)


def _ssm_kernel(*refs, rows_per_seq):
    u_refs, (wu_ref, kslab_ref, mstate_ref, tab_ref) = refs[:S5_BLK], refs[S5_BLK:S5_BLK + 4]
    y_refs = refs[S5_BLK + 4:2 * S5_BLK + 4]
    sre_ref, sim_ref, mintra_ref = refs[2 * S5_BLK + 4:]
    pieces = [r[...] for r in u_refs]
    width = 2 * S5_GROUP
    lane = lax.broadcasted_iota(jnp.int32, kslab_ref.shape[1:], 1)
    outs = []
    for pr in range(LANES // width):
        lanes = slice(pr * width, (pr + 1) * width)
        u = jnp.concatenate([piece[:, lanes] for piece in pieces], axis=1)
        slab = kslab_ref[pr]
        for i in range(S5_BLK):
            shifted = slab if i == 0 else jnp.where(lane >= i * width,
                                                    pltpu.roll(slab, i * width, axis=1), 0.0)
            mintra_ref[i * width:(i + 1) * width, :] = shifted.astype(BF16)
        outs.append(_ssm_pair(u, wu_ref[pr], mintra_ref[...], mstate_ref[pr],
                              [tab_ref[pr, i] for i in range(8)], sre_ref, sim_ref, rows_per_seq))
    for j, y_ref in enumerate(y_refs):
        y_ref[...] = jnp.concatenate([o[:, j * width:(j + 1) * width] for o in outs], axis=1)


def _ssm_pair(u, wu, mintra, mstate, tabs, sre_ref, sim_ref, rows_per_seq):
    v = jnp.dot(u, wu, preferred_element_type=F32)
    sre_ref[...] = v[:, :LANES]
    sim_ref[...] = v[:, LANES:]
    nseq = u.shape[0] // rows_per_seq
    first = lax.broadcasted_iota(jnp.int32, (SUBLANES, LANES), 0) == 0

    def group(g, carry):
        out = []
        for b in range(nseq):
            rows = pl.ds(pl.multiple_of(b * rows_per_seq + g * SUBLANES, SUBLANES), SUBLANES)
            xr, xi = sre_ref[rows, :], sim_ref[rows, :]
            for stage in range(3):
                ar, ai = tabs[2 * stage], tabs[2 * stage + 1]
                sr = pltpu.roll(xr, 1 << stage, axis=0)
                si = pltpu.roll(xi, 1 << stage, axis=0)
                xr, xi = xr + (ar * sr - ai * si), xi + (ar * si + ai * sr)
            cr, ci = carry[2 * b], carry[2 * b + 1]
            xr, xi = xr + (tabs[6] * cr - tabs[7] * ci), xi + (tabs[6] * ci + tabs[7] * cr)
            sre_ref[rows, :] = jnp.where(first, cr, pltpu.roll(xr, 1, axis=0))
            sim_ref[rows, :] = jnp.where(first, ci, pltpu.roll(xi, 1, axis=0))
            out.append(jnp.broadcast_to(xr[SUBLANES - 1:, :], xr.shape))
            out.append(jnp.broadcast_to(xi[SUBLANES - 1:, :], xi.shape))
        return tuple(out)

    zero = jnp.zeros((SUBLANES, LANES), F32)
    lax.fori_loop(0, rows_per_seq // SUBLANES, group, (zero,) * (2 * nseq))
    state = jnp.concatenate([sre_ref[...], sim_ref[...]], axis=1).astype(BF16)
    return (jnp.dot(u, mintra, preferred_element_type=F32)
            + jnp.dot(state, mstate, preferred_element_type=F32)).astype(BF16)


def _ssm_params(lam_re, lam_im, b_re, b_im, c_re, c_im, log_step):
    g, p, h, blk = S5_GROUPS, S5_STATE, S5_GROUP, S5_BLK
    lr, li = lam_re.astype(F32), lam_im.astype(F32)
    step = jnp.exp(log_step.astype(F32))[:, None]
    mag = jnp.exp(lr * step)
    bar_re, bar_im = mag * jnp.cos(li * step), mag * jnp.sin(li * step)
    den = lr * lr + li * li
    f_re = ((bar_re - 1.0) * lr + bar_im * li) / den
    f_im = (bar_im * lr - (bar_re - 1.0) * li) / den
    f_re, f_im = f_re[:, :, None], f_im[:, :, None]
    bb_re = f_re * b_re.astype(F32) - f_im * b_im.astype(F32)
    bb_im = f_re * b_im.astype(F32) + f_im * b_re.astype(F32)
    cr, ci = c_re.astype(F32), c_im.astype(F32)

    def cmul(a, b):
        return a[0] * b[0] - a[1] * b[1], a[0] * b[1] + a[1] * b[0]

    def powers(base, n):
        run = (jnp.ones_like(base[0])[None], jnp.zeros_like(base[0])[None])
        factor = base
        while run[0].shape[0] < n:
            nxt = cmul(run, (factor[0][None], factor[1][None]))
            run = (jnp.concatenate([run[0], nxt[0]]), jnp.concatenate([run[1], nxt[1]]))
            factor = cmul(factor, factor)
        return (jnp.concatenate([run[0], factor[0][None]]),
                jnp.concatenate([run[1], factor[1][None]]))

    npair = g // 2
    w = 2 * blk * h
    exact = functools.partial(jnp.einsum, precision=lax.Precision.HIGHEST)
    pw_re, pw_im = powers((bar_re, bar_im), blk)
    by_pair = lambda m: m.reshape((npair, 2) + m.shape[1:])
    bt_re, bt_im = (by_pair(m.transpose(0, 2, 1)) for m in (bb_re, bb_im))
    ct_re, ct_im = (by_pair(m.transpose(0, 2, 1)) for m in (cr, ci))
    pk_re, pk_im = (by_pair(m.transpose(1, 2, 0)) for m in (pw_re, pw_im))
    same = np.eye(2, dtype=np.float32)

    sel_state = np.einsum('ab,pr->apbr', same, np.eye(p, dtype=np.float32))
    sel_state = np.stack([sel_state, sel_state], axis=2).reshape(2, p, 4 * p)
    part0 = np.concatenate([np.eye(p), np.eye(p), np.zeros((p, 2 * p))], axis=1).astype(np.float32)
    part1 = np.concatenate([np.zeros((p, 2 * p)), np.eye(p), np.eye(p)], axis=1).astype(np.float32)
    rv_re, rv_im = (by_pair(m[:blk][::-1].transpose(1, 0, 2)).transpose(0, 2, 1, 3)
                    for m in (pw_re, pw_im))
    rx_re, rx_im = (exact('niap,apC->niaC', m, sel_state)[:, :, :, None, :] for m in (rv_re, rv_im))
    bx = (exact('nacp,pC->nacC', bt_re, part0) + exact('nacp,pC->nacC', bt_im, part1))[:, None]
    by = (exact('nacp,pC->nacC', bt_re, part1) - exact('nacp,pC->nacC', bt_im, part0))[:, None]
    wu_pair = (rx_re * bx + rx_im * by).reshape(npair, w, 4 * p)

    sel_out = np.einsum('ab,hr->ahbr', same, np.eye(h, dtype=np.float32))
    sel_out = np.tile(sel_out.reshape(2, h, 1, 2 * h), (1, 1, blk, 1)).reshape(2, h, w)
    sel_pos = np.repeat(np.eye(blk + 1, dtype=np.float32)[:, 1:], 2 * h, axis=1)
    cx_re, cx_im = (exact('naph,ahC->napC', m, sel_out) for m in (ct_re, ct_im))
    px_re, px_im = (exact('napk,kC->napC', m, sel_pos) for m in (pk_re, pk_im))
    mstate = jnp.stack([cx_re * px_re - cx_im * px_im, -(cx_re * px_im + cx_im * px_re)], axis=1)
    mstate = mstate.reshape(npair, 4 * p, w)

    cl_re = ct_re[:, :, :, None, :] * pk_re[..., :blk, None] - ct_im[:, :, :, None, :] * pk_im[..., :blk, None]
    cl_im = ct_re[:, :, :, None, :] * pk_im[..., :blk, None] + ct_im[:, :, :, None, :] * pk_re[..., :blk, None]
    kern = exact('napkh,nacp->nackh', cl_re, bt_re) - exact('napkh,nacp->nackh', cl_im, bt_im)
    sel_lag = np.einsum('ab,kl,hr->akhlbr', same, np.eye(blk, dtype=np.float32),
                        np.eye(h, dtype=np.float32)).reshape(2, blk, h, w)
    kslab = exact('nackh,akhC->nacC', kern, sel_lag).reshape(npair, 2 * h, w)

    big = (pw_re[blk].reshape(npair, 1, 2 * p), pw_im[blk].reshape(npair, 1, 2 * p))
    lp_re, lp_im = powers(big, SUBLANES)
    sub = jnp.arange(SUBLANES)[None, :, None]
    rows = []
    for shift in (1, 2, 4):
        for part in (lp_re[shift], lp_im[shift]):
            rows.append(jnp.where(sub >= shift, part, 0.0))
    rows.append(jnp.concatenate(list(lp_re[1:]), axis=1))
    rows.append(jnp.concatenate(list(lp_im[1:]), axis=1))
    tab = jnp.stack(rows, axis=1)
    return wu_pair.astype(BF16), kslab, mstate.astype(BF16), tab


def _ssm(uflat, seq, wu, kslab, mstate, tab):
    nrow = uflat.shape[0]
    ntile = S5_WIDTH // LANES
    ppt = wu.shape[0] // ntile
    piece = lambda j: pl.BlockSpec((nrow, LANES), lambda q: (0, j * ntile + q))
    per_tile = lambda a: pl.BlockSpec((ppt,) + a.shape[1:], lambda q: (q,) + (0,) * (a.ndim - 1))
    return pl.pallas_call(
        functools.partial(_ssm_kernel, rows_per_seq=seq // S5_BLK),
        grid=(ntile,),
        in_specs=([piece(j) for j in range(S5_BLK)]
                  + [per_tile(wu), per_tile(kslab), per_tile(mstate), per_tile(tab)]),
        out_specs=[pl.BlockSpec((nrow, LANES), lambda q: (0, q))] * S5_BLK,
        out_shape=[jax.ShapeDtypeStruct((nrow, S5_WIDTH), BF16)] * S5_BLK,
        scratch_shapes=[pltpu.VMEM((nrow, LANES), F32), pltpu.VMEM((nrow, LANES), F32),
                        pltpu.VMEM((kslab.shape[2], kslab.shape[2]), BF16)],
        compiler_params=pltpu.CompilerParams(
            dimension_semantics=("parallel",), vmem_limit_bytes=VMEM_LIMIT),
        name="s5_blocked",
    )(*([uflat] * S5_BLK), wu, kslab, mstate, tab)


def _merge_kernel(*refs):
    ssm_refs = refs[:S5_BLK]
    (ysb_ref, u_ref, gate_ref, x_ref, d_ref, wglu_ref, wsb_ref, ws5_ref, wout_ref, g_ref, wr_ref,
     br_ref, x1_ref, h2_ref, ek_ref, pk_ref, gk_ref, cnt_ref, carry_ref) = refs[S5_BLK:]

    @pl.when(pl.program_id(0) == 0)
    def _():
        carry_ref[...] = jnp.zeros_like(carry_ref)

    sub = SUB_MERGE
    trow = lax.broadcasted_iota(jnp.int32, (sub, sub), 0)
    tcol = lax.broadcasted_iota(jnp.int32, (sub, sub), 1)
    before = jnp.where(trow < tcol, 1.0, 0.0).astype(BF16)
    erow = lax.broadcasted_iota(jnp.int32, (N_EXPERTS, N_EXPERTS), 0)
    ecol = lax.broadcasted_iota(jnp.int32, (N_EXPERTS, N_EXPERTS), 1)
    lower = jnp.where(ecol < erow, 1.0, 0.0).astype(BF16)
    eidx = lax.broadcasted_iota(jnp.int32, (N_EXPERTS, sub), 0)
    eidx_f = eidx.astype(F32)

    for s in range(x_ref.shape[0] // sub):
        rows = slice(s * sub, (s + 1) * sub)
        nblk = sub // S5_BLK
        by_pos = jnp.concatenate([r[s * nblk:(s + 1) * nblk, :] for r in ssm_refs], axis=0)
        ssm = jnp.dot(_block_order(sub, inverse=True), by_pos, preferred_element_type=F32)
        y5 = jax.nn.gelu(ssm + d_ref[...] * u_ref[rows, :].astype(F32))
        glu = jnp.dot(y5.astype(BF16), wglu_ref[...], preferred_element_type=F32)
        ys5 = (y5 * _sigmoid(glu)).astype(BF16)
        a = jnp.dot(ysb_ref[rows, :], wsb_ref[...], preferred_element_type=F32)
        b = jnp.dot(ys5, ws5_ref[...], preferred_element_type=F32)
        ga = _sigmoid(gate_ref[rows, :D_MODEL].astype(F32))
        gb = _sigmoid(gate_ref[rows, D_MODEL:].astype(F32))
        merged = (ga * a + gb * b).astype(BF16)
        x1 = x_ref[rows, :] + jnp.dot(merged, wout_ref[...], preferred_element_type=F32)
        x1_ref[rows, :] = x1
        h2 = _rmsnorm(x1, g_ref[...])
        h2_ref[rows, :] = h2

        logits = lax.dot_general(wr_ref[...], h2, (((1,), (1,)), ((), ())),
                                 precision=lax.Precision.HIGHEST,
                                 preferred_element_type=F32) + br_ref[...]
        work = logits
        sel = jnp.zeros(logits.shape, jnp.bool_)
        top = None
        for k in range(TOP_K):
            m = jnp.max(work, axis=0, keepdims=True)
            first = jnp.min(jnp.where(work == m, eidx, N_EXPERTS), axis=0, keepdims=True)
            pick = eidx == first
            sel = jnp.logical_or(sel, pick)
            work = jnp.where(pick, -jnp.inf, work)
            if k == 0:
                top = m
        e = jnp.where(sel, jnp.exp(logits - top), 0.0)
        gate = e / jnp.sum(e, axis=0, keepdims=True)

        mask = jnp.where(sel, 1.0, 0.0)
        pos = jnp.dot(mask.astype(BF16), before, preferred_element_type=F32) + carry_ref[...]
        carry_ref[...] += jnp.sum(mask, axis=1, keepdims=True)

        rank = jnp.dot(lower, mask.astype(BF16), preferred_element_type=F32)
        for k in range(TOP_K):
            ind = jnp.logical_and(sel, rank == float(k))
            pick1 = lambda v: jnp.sum(jnp.where(ind, v, 0.0), axis=0, keepdims=True)
            ek_ref[k:k + 1, rows] = pick1(eidx_f).astype(jnp.int32)
            pk_ref[k:k + 1, rows] = pick1(pos).astype(jnp.int32)
            gk_ref[k:k + 1, rows] = pick1(gate)

    cnt_ref[...] = jnp.broadcast_to(carry_ref[...], cnt_ref.shape)


def _merge(ssm_pieces, ysb, u, gates, x2, d_row, wglu, wsb, ws5, wout, g_ffn, wr_t, br_col):
    t = x2.shape[0]
    tm = TM_MERGE
    const = lambda *shape: pl.BlockSpec(shape, lambda i: (0,) * len(shape))
    return pl.pallas_call(
        _merge_kernel,
        grid=(t // tm,),
        in_specs=[pl.BlockSpec((tm // S5_BLK, S5_WIDTH), lambda i: (i, 0))] * S5_BLK + [
            pl.BlockSpec((tm, SB_WIDTH), lambda i: (i, 0)),
            pl.BlockSpec((tm, S5_WIDTH), lambda i: (i, 0)),
            pl.BlockSpec((tm, 2 * D_MODEL), lambda i: (i, 0)),
            pl.BlockSpec((tm, D_MODEL), lambda i: (i, 0)),
            const(1, S5_WIDTH), const(S5_WIDTH, S5_WIDTH),
            const(SB_WIDTH, D_MODEL), const(S5_WIDTH, D_MODEL), const(D_MODEL, D_MODEL),
            const(1, D_MODEL), const(N_EXPERTS, D_MODEL), const(N_EXPERTS, 1),
        ],
        out_specs=[
            pl.BlockSpec((tm, D_MODEL), lambda i: (i, 0)),
            pl.BlockSpec((tm, D_MODEL), lambda i: (i, 0)),
            pl.BlockSpec((TOP_K, tm), lambda i: (0, i)),
            pl.BlockSpec((TOP_K, tm), lambda i: (0, i)),
            pl.BlockSpec((TOP_K, tm), lambda i: (0, i)),
            pl.BlockSpec((N_EXPERTS, LANES), lambda i: (0, 0)),
        ],
        out_shape=[
            jax.ShapeDtypeStruct((t, D_MODEL), F32),
            jax.ShapeDtypeStruct((t, D_MODEL), F32),
            jax.ShapeDtypeStruct((TOP_K, t), jnp.int32),
            jax.ShapeDtypeStruct((TOP_K, t), jnp.int32),
            jax.ShapeDtypeStruct((TOP_K, t), F32),
            jax.ShapeDtypeStruct((N_EXPERTS, LANES), F32),
        ],
        scratch_shapes=[pltpu.VMEM((N_EXPERTS, 1), F32)],
        compiler_params=pltpu.CompilerParams(
            dimension_semantics=("arbitrary",), vmem_limit_bytes=VMEM_LIMIT),
        name="merge_router",
    )(*ssm_pieces, ysb, u, gates, x2, d_row, wglu, wsb, ws5, wout, g_ffn, wr_t, br_col)


def _layout(ek, pk, t):
    tm = TM_DISPATCH
    ni = t // tm
    ek3, pk3 = ek.reshape(TOP_K, ni, tm), pk.reshape(TOP_K, ni, tm)
    cnt = jnp.stack([jnp.sum(ek3 == e, axis=(0, 2)) for e in range(N_EXPERTS)], axis=1)
    cnt = cnt.astype(jnp.int32)
    seg = (cnt + SUBLANES - 1) // SUBLANES * SUBLANES
    group = jnp.sum(seg, axis=0)
    padded = (group + TR - 1) // TR * TR
    ends = jnp.cumsum(padded)
    dst = (ends - padded)[None, :] + jnp.cumsum(seg, axis=0) - seg
    stage = jnp.cumsum(seg, axis=1) - seg
    base = stage - (jnp.cumsum(cnt, axis=0) - cnt)
    slot = pk3
    for e in range(N_EXPERTS):
        slot = slot + jnp.where(ek3 == e, base[None, :, e, None], 0)
    ntile = (TOP_K * t + ni * N_EXPERTS * (SUBLANES - 1)) // TR + 1 + N_EXPERTS
    nused = ends[-1] // TR
    tile_start = jnp.arange(ntile, dtype=jnp.int32) * TR
    tile_expert = jnp.sum(tile_start[:, None] >= ends[None, :], axis=1).astype(jnp.int32)
    tile_expert = jnp.minimum(tile_expert, jnp.take(tile_expert, nused - 1))
    changes = jnp.concatenate([jnp.zeros((1,), jnp.int32),
                               (tile_expert[1:] != tile_expert[:-1]).astype(jnp.int32)])
    tile_parity = jnp.cumsum(changes) % 2
    tail_tile = jnp.maximum(ends // TR - 1, 0).astype(jnp.int32)
    segs = (stage.reshape(-1), dst.reshape(-1).astype(jnp.int32), seg.reshape(-1))
    return (slot.reshape(TOP_K, t), segs, group.astype(jnp.int32), tail_tile, tile_expert,
            tile_parity.astype(jnp.int32), nused.reshape(1).astype(jnp.int32), ntile)


def _segment_copy(stage_ref, dst_ref, len_ref, idx, staging, buffer, sem, to_buffer):
    aligned = lambda v: pl.multiple_of(v, SUBLANES)
    n = aligned(len_ref[idx])
    a = staging.at[pl.ds(aligned(stage_ref[idx]), n), :]
    b = buffer.at[pl.ds(aligned(dst_ref[idx]), n), :]
    return pltpu.make_async_copy(a, b, sem) if to_buffer else pltpu.make_async_copy(b, a, sem)


def _dispatch_kernel(stage_ref, dst_ref, len_ref, grp_ref, tail_ref, nused_ref, slot_ref, h2_ref,
                     xg_hbm, s_ref, zero_ref, sem, zsem):
    i = pl.program_id(0)

    @pl.when(i == 0)
    def _():
        zero_ref[...] = jnp.zeros_like(zero_ref)
        fill = lambda e: pltpu.make_async_copy(
            zero_ref, xg_hbm.at[pl.ds(pl.multiple_of(tail_ref[e] * TR, TR), TR), :], zsem)
        for e in range(N_EXPERTS):
            @pl.when(grp_ref[e] > 0)
            def _():
                fill(e).start()
        for e in range(N_EXPERTS):
            @pl.when(grp_ref[e] > 0)
            def _():
                fill(e).wait()

        spare = lambda r: pltpu.make_async_copy(
            zero_ref, xg_hbm.at[pl.ds(pl.multiple_of(r * TR, TR), TR), :], zsem)
        ntile = xg_hbm.shape[0] // TR
        lax.fori_loop(nused_ref[0], ntile, lambda r, c: (spare(r).start(), c)[1], 0)
        lax.fori_loop(nused_ref[0], ntile, lambda r, c: (spare(r).wait(), c)[1], 0)

    rowid = lax.broadcasted_iota(jnp.int32, (s_ref.shape[1], h2_ref.shape[0]), 0)
    hit = slot_ref[0:1, :] == rowid
    for k in range(1, TOP_K):
        hit = jnp.logical_or(hit, slot_ref[k:k + 1, :] == rowid)
    onehot = jnp.where(hit, 1.0, 0.0).astype(BF16)
    buf = lax.rem(i, 2)
    s_ref[buf] = jnp.dot(onehot, h2_ref[...].astype(BF16), preferred_element_type=F32)

    def each_segment(tile, buf, action):
        def body(e, carry):
            idx = tile * N_EXPERTS + e

            @pl.when(len_ref[idx] > 0)
            def _():
                action(_segment_copy(stage_ref, dst_ref, len_ref, idx, s_ref.at[buf], xg_hbm,
                                     sem.at[buf], True))
            return carry
        lax.fori_loop(0, N_EXPERTS, body, 0)

    each_segment(i, buf, lambda copy: copy.start())

    @pl.when(i >= 1)
    def _():
        each_segment(i - 1, 1 - buf, lambda copy: copy.wait())

    @pl.when(i == pl.num_programs(0) - 1)
    def _():
        each_segment(i, buf, lambda copy: copy.wait())


def _dispatch(slot, segs, group, tail_tile, nused, h2, nrow):
    t = h2.shape[0]
    tm = TM_DISPATCH
    grid_spec = pltpu.PrefetchScalarGridSpec(
        num_scalar_prefetch=6,
        grid=(t // tm,),
        in_specs=[pl.BlockSpec((TOP_K, tm), lambda i, *_: (0, i)),
                  pl.BlockSpec((tm, D_MODEL), lambda i, *_: (i, 0))],
        out_specs=pl.BlockSpec(memory_space=pl.ANY),
        scratch_shapes=[pltpu.VMEM((2, STAGE_ROWS, D_MODEL), F32), pltpu.VMEM((TR, D_MODEL), F32),
                        pltpu.SemaphoreType.DMA((2,)), pltpu.SemaphoreType.DMA(())],
    )
    return pl.pallas_call(
        _dispatch_kernel,
        grid_spec=grid_spec,
        out_shape=jax.ShapeDtypeStruct((nrow, D_MODEL), F32),
        compiler_params=pltpu.CompilerParams(
            dimension_semantics=("arbitrary",), vmem_limit_bytes=VMEM_LIMIT),
        name="moe_dispatch",
    )(*segs, group, tail_tile, nused, slot, h2)


def _experts_kernel(te_ref, par_ref, nused_ref, xg_ref, wup_ref, bup_ref, wdn_ref, bdn_ref, y_ref,
                    wup_bf, wdn_bf):
    s = pl.program_id(0)
    nused = nused_ref[0]
    r = s - 1

    @pl.when(jnp.logical_and(r >= 0, r < nused))
    def _():
        slot = par_ref[jnp.maximum(r, 0)]
        x = xg_ref[...].astype(BF16)
        hid = jnp.dot(x, wup_bf[slot], preferred_element_type=F32) + bup_ref[0]
        glu = jnp.minimum(hid[:, :D_FF], SWIGLU_LIMIT)
        lin = jnp.clip(hid[:, D_FF:], -SWIGLU_LIMIT, SWIGLU_LIMIT)
        act = glu * _sigmoid(SWIGLU_ALPHA * glu) * (lin + 1.0)
        y_ref[...] = jnp.dot(act.astype(BF16), wdn_bf[slot], preferred_element_type=F32) + bdn_ref[0]

    @pl.when(r >= nused)
    def _():
        y_ref[...] = jnp.zeros_like(y_ref)

    last = te_ref.shape[0] - 1
    cur = jnp.minimum(s, last)
    new_expert = jnp.logical_or(s == 0, te_ref[cur] != te_ref[jnp.maximum(cur - 1, 0)])

    @pl.when(jnp.logical_and(s < nused, new_expert))
    def _():
        slot = par_ref[cur]
        wup_bf[slot] = wup_ref[0].astype(BF16)
        wdn_bf[slot] = wdn_ref[0].astype(BF16)


def _experts(tile_expert, tile_parity, nused, xg, w_up, b_up, w_down, b_down):
    ntile = tile_expert.shape[0]
    ahead = lambda s, te, par, nu: (te[jnp.minimum(s, ntile - 1)], 0, 0)
    current = lambda s, te, par, nu: (te[jnp.maximum(s - 1, 0)], 0, 0)
    by_block = lambda s, te, par, nu: (jnp.maximum(s - 1, 0), 0)
    grid_spec = pltpu.PrefetchScalarGridSpec(
        num_scalar_prefetch=3,
        grid=(ntile + 1,),
        in_specs=[
            pl.BlockSpec((TR, D_MODEL), by_block),
            pl.BlockSpec((1, D_MODEL, 2 * D_FF), ahead),
            pl.BlockSpec((1, 1, 2 * D_FF), current),
            pl.BlockSpec((1, D_FF, D_MODEL), ahead),
            pl.BlockSpec((1, 1, D_MODEL), current),
        ],
        out_specs=pl.BlockSpec((TR, D_MODEL), by_block),
        scratch_shapes=[pltpu.VMEM((2, D_MODEL, 2 * D_FF), BF16),
                        pltpu.VMEM((2, D_FF, D_MODEL), BF16)],
    )
    return pl.pallas_call(
        _experts_kernel,
        grid_spec=grid_spec,
        out_shape=jax.ShapeDtypeStruct(xg.shape, F32),
        compiler_params=pltpu.CompilerParams(
            dimension_semantics=("arbitrary",), vmem_limit_bytes=VMEM_LIMIT),
        name="moe_experts",
    )(tile_expert, tile_parity, nused, xg, w_up, b_up, w_down, b_down)


def _gather_combine_kernel(stage_ref, dst_ref, len_ref, slot_ref, gk_ref, x1_ref, gfin_ref, y_hbm,
                           o_ref, sbuf, sem):
    i = pl.program_id(0)
    nstep = pl.num_programs(0)

    def each_segment(tile, buf, action):
        def body(e, carry):
            idx = tile * N_EXPERTS + e

            @pl.when(len_ref[idx] > 0)
            def _():
                action(_segment_copy(stage_ref, dst_ref, len_ref, idx, sbuf.at[buf], y_hbm,
                                     sem.at[buf], False))
            return carry
        lax.fori_loop(0, N_EXPERTS, body, 0)

    @pl.when(i == 0)
    def _():
        sbuf[...] = jnp.zeros_like(sbuf)
        each_segment(0, 0, lambda copy: copy.start())

    @pl.when(i + 1 < nstep)
    def _():
        each_segment(i + 1, lax.rem(i + 1, 2), lambda copy: copy.start())

    buf = lax.rem(i, 2)
    each_segment(i, buf, lambda copy: copy.wait())

    lane = lax.broadcasted_iota(jnp.int32, (x1_ref.shape[0], sbuf.shape[1]), 1)
    weight = jnp.where(slot_ref[0] == lane, gk_ref[0], 0.0)
    for k in range(1, TOP_K):
        weight = weight + jnp.where(slot_ref[k] == lane, gk_ref[k], 0.0)
    hi = weight.astype(BF16)
    lo = (weight - hi.astype(F32)).astype(BF16)
    rows = sbuf[buf].astype(BF16)
    acc = (x1_ref[...] + jnp.dot(hi, rows, preferred_element_type=F32)
           + jnp.dot(lo, rows, preferred_element_type=F32))
    o_ref[...] = _rmsnorm(acc, gfin_ref[...])


def _gather_combine(slot_col, segs, x1, gk_col, g_fin, y):
    t = x1.shape[0]
    tm = TM_DISPATCH
    grid_spec = pltpu.PrefetchScalarGridSpec(
        num_scalar_prefetch=3,
        grid=(t // tm,),
        in_specs=[
            pl.BlockSpec((TOP_K, tm, 1), lambda i, *_: (0, i, 0)),
            pl.BlockSpec((TOP_K, tm, 1), lambda i, *_: (0, i, 0)),
            pl.BlockSpec((tm, D_MODEL), lambda i, *_: (i, 0)),
            pl.BlockSpec((1, D_MODEL), lambda i, *_: (0, 0)),
            pl.BlockSpec(memory_space=pl.ANY),
        ],
        out_specs=pl.BlockSpec((tm, D_MODEL), lambda i, *_: (i, 0)),
        scratch_shapes=[pltpu.VMEM((2, STAGE_ROWS, D_MODEL), F32), pltpu.SemaphoreType.DMA((2,))],
    )
    return pl.pallas_call(
        _gather_combine_kernel,
        grid_spec=grid_spec,
        out_shape=jax.ShapeDtypeStruct((t, D_MODEL), F32),
        compiler_params=pltpu.CompilerParams(
            dimension_semantics=("arbitrary",), vmem_limit_bytes=VMEM_LIMIT),
        name="gather_combine_norm",
    )(*segs, slot_col, gk_col, x1, g_fin, y)


def kernel(x, norm_mix, w_in, s5_lam_re, s5_lam_im, s5_b_re, s5_b_im, s5_c_re, s5_c_im, s5_d,
           s5_log_step, w_glu, w_branch_sb, w_branch_s5, w_out, norm_ffn, w_router, b_router,
           w_up, b_up, w_down, b_down, norm_final):
    bsz, seq, d = x.shape
    t = bsz * seq
    assert norm_mix.shape[0] == 1, "single-layer block"
    l = 0
    x2 = x.reshape(t, d).astype(F32)
    qkv, u, ublk, gates = _inproj(x2, norm_mix[l].reshape(1, d).astype(F32), w_in[l].astype(BF16))
    ysb = _attention(qkv.reshape(bsz, seq, 3 * SB_WIDTH))
    ssm = _ssm(ublk, seq, *_ssm_params(s5_lam_re[l], s5_lam_im[l], s5_b_re[l], s5_b_im[l],
                                       s5_c_re[l], s5_c_im[l], s5_log_step[l]))
    x1, h2, ek, pk, gk, cnt = _merge(
        ssm, ysb.reshape(t, SB_WIDTH), u, gates, x2,
        s5_d[l].reshape(1, S5_WIDTH).astype(F32), w_glu[l].astype(BF16),
        w_branch_sb[l].astype(BF16), w_branch_s5[l].astype(BF16), w_out[l].astype(BF16),
        norm_ffn[l].reshape(1, d).astype(F32), w_router[l].T.astype(F32),
        b_router[l].reshape(N_EXPERTS, 1).astype(F32))
    slot, segs, group, tail_tile, tile_expert, tile_parity, nused, ntile = _layout(ek, pk, t)
    xg = _dispatch(slot, segs, group, tail_tile, nused, h2, ntile * TR)
    y = _experts(tile_expert, tile_parity, nused, xg, w_up[l],
                 b_up[l].reshape(N_EXPERTS, 1, 2 * D_FF), w_down[l],
                 b_down[l].reshape(N_EXPERTS, 1, D_MODEL))
    out = _gather_combine(slot.reshape(TOP_K, t, 1), segs, x1, gk.reshape(TOP_K, t, 1),
                          norm_final.reshape(1, d).astype(F32), y)
    return out.reshape(bsz, seq, d).astype(x.dtype)
```

```python
import functools

import jax
import jax.numpy as jnp
import numpy as np
from jax import lax
from jax.experimental import pallas as pl
from jax.experimental.pallas import tpu as pltpu

F32 = jnp.float32
BF16 = jnp.bfloat16

D_MODEL = 1024
SB_HEADS = 8
SB_HEAD_DIM = 64
SB_WIDTH = SB_HEADS * SB_HEAD_DIM
S5_WIDTH = 512
S5_GROUP = 16
S5_GROUPS = 32
S5_STATE = 64
N_EXPERTS = 32
TOP_K = 4
D_FF = 1024
SWIGLU_LIMIT = 7.0
SWIGLU_ALPHA = 1.702
RMS_EPS = 1e-5
OFF_U = 3 * SB_WIDTH
OFF_GATES = OFF_U + S5_WIDTH
IN_WIDTH = OFF_GATES + 2 * D_MODEL

LANES = 128
SUBLANES = 8
VMEM_LIMIT = 56 * 1024 * 1024

TM_PROJ = 512
TQ = 256
TK = 256
Q_PER_STEP = 8
EXP_UNDERFLOW = 110.0
S5_BLK = 16
TM_MERGE = 512
SUB_MERGE = 512
TR = 512
TM_DISPATCH = 256
STAGE_ROWS = TOP_K * TM_DISPATCH + N_EXPERTS * SUBLANES


def _rmsnorm(x, g):
    return x * lax.rsqrt(jnp.mean(x * x, axis=-1, keepdims=True) + RMS_EPS) * g


def _sigmoid(x):
    return 1.0 / (1.0 + jnp.exp(-x))


def _block_order(tm, inverse=False):
    row = lax.broadcasted_iota(jnp.int32, (tm, tm), 0)
    col = lax.broadcasted_iota(jnp.int32, (tm, tm), 1)
    nblk = tm // S5_BLK
    if inverse:
        return jnp.where(col == (row % S5_BLK) * nblk + row // S5_BLK, 1.0, 0.0).astype(BF16)
    return jnp.where(col == (row % nblk) * S5_BLK + row // nblk, 1.0, 0.0).astype(BF16)


def _inproj_kernel(x_ref, g_ref, w_ref, qkv_ref, u_ref, ublk_ref, gate_ref):
    hb = _rmsnorm(x_ref[...], g_ref[...]).astype(BF16)
    chunk = SB_WIDTH

    def proj(c0):
        return jnp.dot(hb, w_ref[:, c0:c0 + chunk], preferred_element_type=F32)

    qkv_ref[:, 0:chunk] = (proj(0) * (SB_HEAD_DIM ** -0.5)).astype(BF16)
    qkv_ref[:, chunk:2 * chunk] = proj(chunk).astype(BF16)
    qkv_ref[:, 2 * chunk:3 * chunk] = proj(2 * chunk).astype(BF16)
    u = proj(OFF_U).astype(BF16)
    u_ref[...] = u
    tm = u.shape[0]
    nblk = tm // S5_BLK
    by_pos = jnp.dot(_block_order(tm), u, preferred_element_type=F32).astype(BF16)
    for j in range(S5_BLK):
        ublk_ref[:, j * S5_WIDTH:(j + 1) * S5_WIDTH] = by_pos[j * nblk:(j + 1) * nblk, :]
    for c in range(2 * D_MODEL // chunk):
        gate_ref[:, c * chunk:(c + 1) * chunk] = proj(OFF_GATES + c * chunk).astype(BF16)


def _inproj(x2, g, w_bf):
    t = x2.shape[0]
    return pl.pallas_call(
        _inproj_kernel,
        grid=(t // TM_PROJ,),
        in_specs=[
            pl.BlockSpec((TM_PROJ, D_MODEL), lambda i: (i, 0)),
            pl.BlockSpec((1, D_MODEL), lambda i: (0, 0)),
            pl.BlockSpec((D_MODEL, IN_WIDTH), lambda i: (0, 0)),
        ],
        out_specs=[
            pl.BlockSpec((TM_PROJ, 3 * SB_WIDTH), lambda i: (i, 0)),
            pl.BlockSpec((TM_PROJ, S5_WIDTH), lambda i: (i, 0)),
            pl.BlockSpec((TM_PROJ // S5_BLK, S5_BLK * S5_WIDTH), lambda i: (i, 0)),
            pl.BlockSpec((TM_PROJ, 2 * D_MODEL), lambda i: (i, 0)),
        ],
        out_shape=[
            jax.ShapeDtypeStruct((t, 3 * SB_WIDTH), BF16),
            jax.ShapeDtypeStruct((t, S5_WIDTH), BF16),
            jax.ShapeDtypeStruct((t // S5_BLK, S5_BLK * S5_WIDTH), BF16),
            jax.ShapeDtypeStruct((t, 2 * D_MODEL), BF16),
        ],
        compiler_params=pltpu.CompilerParams(
            dimension_semantics=("parallel",), vmem_limit_bytes=VMEM_LIMIT),
        name="inproj",
    )(x2, g, w_bf)


def _attn_kernel(q_ref, k_ref, v_ref, o_ref, acc_ref, cb_ref):
    step = pl.program_id(2)
    nheads = LANES // SB_HEAD_DIM
    is_h0 = lax.broadcasted_iota(jnp.int32, (TQ, LANES), 1) < SB_HEAD_DIM
    row = lax.broadcasted_iota(jnp.int32, (TQ, TK), 0)
    col = lax.broadcasted_iota(jnp.int32, (TQ, TK), 1)
    tri = jnp.where(row >= col, 1.0, 0.0).astype(BF16)
    causal = col < row
    acc_ref[...] = jnp.zeros_like(acc_ref)
    cb_ref[...] = jnp.zeros_like(cb_ref)

    def scores(qm, k, mask):
        z = lax.dot_general(qm, k, (((1,), (1,)), ((), ())), preferred_element_type=F32)
        log_keep = -(jnp.maximum(z, 0.0) + jnp.log(1.0 + jnp.exp(-jnp.abs(z))))
        if mask is not None:
            log_keep = jnp.where(mask, log_keep, 0.0)
        hi = log_keep.astype(BF16)
        lo = (log_keep - hi.astype(F32)).astype(BF16)
        rcum = (jnp.dot(hi, tri, preferred_element_type=F32)
                + jnp.dot(lo, tri, preferred_element_type=F32))
        return z, rcum

    def pair(sub, qms, kt, diag):
        has_b = kt >= 1
        ka = pl.multiple_of(kt * TK, TK)
        kb = pl.multiple_of(jnp.maximum(kt - 1, 0) * TK, TK)
        k_a, v_a = k_ref[0, pl.ds(ka, TK), :], v_ref[0, pl.ds(ka, TK), :]
        k_b, v_b = k_ref[0, pl.ds(kb, TK), :], v_ref[0, pl.ds(kb, TK), :]
        mask_a = causal if diag else None
        pvs, worst = [], None
        for h, qm in enumerate(qms):
            z_a, rcum_a = scores(qm, k_a, mask_a)
            z_b, rcum_b = scores(qm, k_b, has_b)
            cb = cb_ref[nheads * sub + h]
            cb_a = cb + rcum_a[:, 0:1]
            w_a = jnp.exp(z_a + rcum_a + cb)
            if diag:
                w_a = jnp.where(causal, w_a, 0.0)
            w_b = jnp.where(has_b, jnp.exp(z_b + rcum_b + cb_a), 0.0)
            pvs.append(jnp.dot(w_a.astype(BF16), v_a, preferred_element_type=F32)
                       + jnp.dot(w_b.astype(BF16), v_b, preferred_element_type=F32))
            cb_new = cb_a + rcum_b[:, 0:1]
            cb_ref[nheads * sub + h] = cb_new
            top = jnp.max(cb_new)
            worst = top if worst is None else jnp.maximum(worst, top)
        acc_ref[sub] += jnp.where(is_h0, pvs[0], pvs[1])
        return jnp.logical_not(worst < -EXP_UNDERFLOW)

    tiles = []
    for sub in range(Q_PER_STEP):
        q = q_ref[0, sub * TQ:(sub + 1) * TQ, :]
        zero = jnp.zeros_like(q)
        qms = (jnp.where(is_h0, q, zero), jnp.where(is_h0, zero, q))
        qi = step * Q_PER_STEP + sub
        tiles.append((qms, qi, pair(sub, qms, qi, True)))

    for sub, (qms, qi, live) in enumerate(tiles):
        def cond(carry):
            kt, live = carry
            return jnp.logical_and(kt >= 0, live)

        def body(carry, sub=sub, qms=qms):
            kt, _ = carry
            return kt - 2, pair(sub, qms, kt, False)

        lax.while_loop(cond, body, (qi - 2, live))
        o_ref[0, sub * TQ:(sub + 1) * TQ, :] = acc_ref[sub].astype(BF16)


def _attention(qkv3):
    b, l, _ = qkv3.shape
    pairs = SB_WIDTH // LANES
    tq = Q_PER_STEP * TQ
    nheads = LANES // SB_HEAD_DIM
    return pl.pallas_call(
        _attn_kernel,
        grid=(b, pairs, l // tq),
        in_specs=[
            pl.BlockSpec((1, tq, LANES), lambda bi, hp, qi: (bi, qi, hp)),
            pl.BlockSpec((1, l, LANES), lambda bi, hp, qi: (bi, 0, pairs + hp)),
            pl.BlockSpec((1, l, LANES), lambda bi, hp, qi: (bi, 0, 2 * pairs + hp)),
        ],
        out_specs=pl.BlockSpec((1, tq, LANES), lambda bi, hp, qi: (bi, qi, hp)),
        out_shape=jax.ShapeDtypeStruct((b, l, SB_WIDTH), BF16),
        scratch_shapes=[pltpu.VMEM((Q_PER_STEP, TQ, LANES), F32),
                        pltpu.VMEM((Q_PER_STEP * nheads, TQ, 1), F32)],
        compiler_params=pltpu.CompilerParams(
            dimension_semantics=("parallel", "parallel", "arbitrary"),
            vmem_limit_bytes=VMEM_LIMIT),
        name="sb_attention",
    )(qkv3, qkv3, qkv3)


def _ssm_kernel(*refs, rows_per_seq):
    u_refs, (wu_ref, kslab_ref, mstate_ref, tab_ref) = refs[:S5_BLK], refs[S5_BLK:S5_BLK + 4]
    y_refs = refs[S5_BLK + 4:2 * S5_BLK + 4]
    sre_ref, sim_ref, mintra_ref = refs[2 * S5_BLK + 4:]
    pieces = [r[...] for r in u_refs]
    width = 2 * S5_GROUP
    lane = lax.broadcasted_iota(jnp.int32, kslab_ref.shape[1:], 1)
    outs = []
    for pr in range(LANES // width):
        lanes = slice(pr * width, (pr + 1) * width)
        u = jnp.concatenate([piece[:, lanes] for piece in pieces], axis=1)
        slab = kslab_ref[pr]
        for i in range(S5_BLK):
            shifted = slab if i == 0 else jnp.where(lane >= i * width,
                                                    pltpu.roll(slab, i * width, axis=1), 0.0)
            mintra_ref[i * width:(i + 1) * width, :] = shifted.astype(BF16)
        outs.append(_ssm_pair(u, wu_ref[pr], mintra_ref[...], mstate_ref[pr],
                              [tab_ref[pr, i] for i in range(8)], sre_ref, sim_ref, rows_per_seq))
    for j, y_ref in enumerate(y_refs):
        y_ref[...] = jnp.concatenate([o[:, j * width:(j + 1) * width] for o in outs], axis=1)


def _ssm_pair(u, wu, mintra, mstate, tabs, sre_ref, sim_ref, rows_per_seq):
    v = jnp.dot(u, wu, preferred_element_type=F32)
    sre_ref[...] = v[:, :LANES]
    sim_ref[...] = v[:, LANES:]
    nseq = u.shape[0] // rows_per_seq
    first = lax.broadcasted_iota(jnp.int32, (SUBLANES, LANES), 0) == 0

    def group(g, carry):
        out = []
        for b in range(nseq):
            rows = pl.ds(pl.multiple_of(b * rows_per_seq + g * SUBLANES, SUBLANES), SUBLANES)
            xr, xi = sre_ref[rows, :], sim_ref[rows, :]
            for stage in range(3):
                ar, ai = tabs[2 * stage], tabs[2 * stage + 1]
                sr = pltpu.roll(xr, 1 << stage, axis=0)
                si = pltpu.roll(xi, 1 << stage, axis=0)
                xr, xi = xr + (ar * sr - ai * si), xi + (ar * si + ai * sr)
            cr, ci = carry[2 * b], carry[2 * b + 1]
            xr, xi = xr + (tabs[6] * cr - tabs[7] * ci), xi + (tabs[6] * ci + tabs[7] * cr)
            sre_ref[rows, :] = jnp.where(first, cr, pltpu.roll(xr, 1, axis=0))
            sim_ref[rows, :] = jnp.where(first, ci, pltpu.roll(xi, 1, axis=0))
            out.append(jnp.broadcast_to(xr[SUBLANES - 1:, :], xr.shape))
            out.append(jnp.broadcast_to(xi[SUBLANES - 1:, :], xi.shape))
        return tuple(out)

    zero = jnp.zeros((SUBLANES, LANES), F32)
    lax.fori_loop(0, rows_per_seq // SUBLANES, group, (zero,) * (2 * nseq))
    state = jnp.concatenate([sre_ref[...], sim_ref[...]], axis=1).astype(BF16)
    return (jnp.dot(u, mintra, preferred_element_type=F32)
            + jnp.dot(state, mstate, preferred_element_type=F32)).astype(BF16)


def _ssm_params(lam_re, lam_im, b_re, b_im, c_re, c_im, log_step):
    g, p, h, blk = S5_GROUPS, S5_STATE, S5_GROUP, S5_BLK
    lr, li = lam_re.astype(F32), lam_im.astype(F32)
    step = jnp.exp(log_step.astype(F32))[:, None]
    mag = jnp.exp(lr * step)
    bar_re, bar_im = mag * jnp.cos(li * step), mag * jnp.sin(li * step)
    den = lr * lr + li * li
    f_re = ((bar_re - 1.0) * lr + bar_im * li) / den
    f_im = (bar_im * lr - (bar_re - 1.0) * li) / den
    f_re, f_im = f_re[:, :, None], f_im[:, :, None]
    bb_re = f_re * b_re.astype(F32) - f_im * b_im.astype(F32)
    bb_im = f_re * b_im.astype(F32) + f_im * b_re.astype(F32)
    cr, ci = c_re.astype(F32), c_im.astype(F32)

    def cmul(a, b):
        return a[0] * b[0] - a[1] * b[1], a[0] * b[1] + a[1] * b[0]

    def powers(base, n):
        run = (jnp.ones_like(base[0])[None], jnp.zeros_like(base[0])[None])
        factor = base
        while run[0].shape[0] < n:
            nxt = cmul(run, (factor[0][None], factor[1][None]))
            run = (jnp.concatenate([run[0], nxt[0]]), jnp.concatenate([run[1], nxt[1]]))
            factor = cmul(factor, factor)
        return (jnp.concatenate([run[0], factor[0][None]]),
                jnp.concatenate([run[1], factor[1][None]]))

    npair = g // 2
    w = 2 * blk * h
    exact = functools.partial(jnp.einsum, precision=lax.Precision.HIGHEST)
    pw_re, pw_im = powers((bar_re, bar_im), blk)
    by_pair = lambda m: m.reshape((npair, 2) + m.shape[1:])
    bt_re, bt_im = (by_pair(m.transpose(0, 2, 1)) for m in (bb_re, bb_im))
    ct_re, ct_im = (by_pair(m.transpose(0, 2, 1)) for m in (cr, ci))
    pk_re, pk_im = (by_pair(m.transpose(1, 2, 0)) for m in (pw_re, pw_im))
    same = np.eye(2, dtype=np.float32)

    sel_state = np.einsum('ab,pr->apbr', same, np.eye(p, dtype=np.float32))
    sel_state = np.stack([sel_state, sel_state], axis=2).reshape(2, p, 4 * p)
    part0 = np.concatenate([np.eye(p), np.eye(p), np.zeros((p, 2 * p))], axis=1).astype(np.float32)
    part1 = np.concatenate([np.zeros((p, 2 * p)), np.eye(p), np.eye(p)], axis=1).astype(np.float32)
    rv_re, rv_im = (by_pair(m[:blk][::-1].transpose(1, 0, 2)).transpose(0, 2, 1, 3)
                    for m in (pw_re, pw_im))
    rx_re, rx_im = (exact('niap,apC->niaC', m, sel_state)[:, :, :, None, :] for m in (rv_re, rv_im))
    bx = (exact('nacp,pC->nacC', bt_re, part0) + exact('nacp,pC->nacC', bt_im, part1))[:, None]
    by = (exact('nacp,pC->nacC', bt_re, part1) - exact('nacp,pC->nacC', bt_im, part0))[:, None]
    wu_pair = (rx_re * bx + rx_im * by).reshape(npair, w, 4 * p)

    sel_out = np.einsum('ab,hr->ahbr', same, np.eye(h, dtype=np.float32))
    sel_out = np.tile(sel_out.reshape(2, h, 1, 2 * h), (1, 1, blk, 1)).reshape(2, h, w)
    sel_pos = np.repeat(np.eye(blk + 1, dtype=np.float32)[:, 1:], 2 * h, axis=1)
    cx_re, cx_im = (exact('naph,ahC->napC', m, sel_out) for m in (ct_re, ct_im))
    px_re, px_im = (exact('napk,kC->napC', m, sel_pos) for m in (pk_re, pk_im))
    mstate = jnp.stack([cx_re * px_re - cx_im * px_im, -(cx_re * px_im + cx_im * px_re)], axis=1)
    mstate = mstate.reshape(npair, 4 * p, w)

    cl_re = ct_re[:, :, :, None, :] * pk_re[..., :blk, None] - ct_im[:, :, :, None, :] * pk_im[..., :blk, None]
    cl_im = ct_re[:, :, :, None, :] * pk_im[..., :blk, None] + ct_im[:, :, :, None, :] * pk_re[..., :blk, None]
    kern = exact('napkh,nacp->nackh', cl_re, bt_re) - exact('napkh,nacp->nackh', cl_im, bt_im)
    sel_lag = np.einsum('ab,kl,hr->akhlbr', same, np.eye(blk, dtype=np.float32),
                        np.eye(h, dtype=np.float32)).reshape(2, blk, h, w)
    kslab = exact('nackh,akhC->nacC', kern, sel_lag).reshape(npair, 2 * h, w)

    big = (pw_re[blk].reshape(npair, 1, 2 * p), pw_im[blk].reshape(npair, 1, 2 * p))
    lp_re, lp_im = powers(big, SUBLANES)
    sub = jnp.arange(SUBLANES)[None, :, None]
    rows = []
    for shift in (1, 2, 4):
        for part in (lp_re[shift], lp_im[shift]):
            rows.append(jnp.where(sub >= shift, part, 0.0))
    rows.append(jnp.concatenate(list(lp_re[1:]), axis=1))
    rows.append(jnp.concatenate(list(lp_im[1:]), axis=1))
    tab = jnp.stack(rows, axis=1)
    return wu_pair.astype(BF16), kslab, mstate.astype(BF16), tab


def _ssm(uflat, seq, wu, kslab, mstate, tab):
    nrow = uflat.shape[0]
    ntile = S5_WIDTH // LANES
    ppt = wu.shape[0] // ntile
    piece = lambda j: pl.BlockSpec((nrow, LANES), lambda q: (0, j * ntile + q))
    per_tile = lambda a: pl.BlockSpec((ppt,) + a.shape[1:], lambda q: (q,) + (0,) * (a.ndim - 1))
    return pl.pallas_call(
        functools.partial(_ssm_kernel, rows_per_seq=seq // S5_BLK),
        grid=(ntile,),
        in_specs=([piece(j) for j in range(S5_BLK)]
                  + [per_tile(wu), per_tile(kslab), per_tile(mstate), per_tile(tab)]),
        out_specs=[pl.BlockSpec((nrow, LANES), lambda q: (0, q))] * S5_BLK,
        out_shape=[jax.ShapeDtypeStruct((nrow, S5_WIDTH), BF16)] * S5_BLK,
        scratch_shapes=[pltpu.VMEM((nrow, LANES), F32), pltpu.VMEM((nrow, LANES), F32),
                        pltpu.VMEM((kslab.shape[2], kslab.shape[2]), BF16)],
        compiler_params=pltpu.CompilerParams(
            dimension_semantics=("parallel",), vmem_limit_bytes=VMEM_LIMIT),
        name="s5_blocked",
    )(*([uflat] * S5_BLK), wu, kslab, mstate, tab)


def _merge_kernel(*refs):
    ssm_refs = refs[:S5_BLK]
    (ysb_ref, u_ref, gate_ref, x_ref, d_ref, wglu_ref, wsb_ref, ws5_ref, wout_ref, g_ref, wr_ref,
     br_ref, x1_ref, h2_ref, ek_ref, pk_ref, gk_ref, cnt_ref, carry_ref) = refs[S5_BLK:]

    @pl.when(pl.program_id(0) == 0)
    def _():
        carry_ref[...] = jnp.zeros_like(carry_ref)

    sub = SUB_MERGE
    trow = lax.broadcasted_iota(jnp.int32, (sub, sub), 0)
    tcol = lax.broadcasted_iota(jnp.int32, (sub, sub), 1)
    before = jnp.where(trow < tcol, 1.0, 0.0).astype(BF16)
    erow = lax.broadcasted_iota(jnp.int32, (N_EXPERTS, N_EXPERTS), 0)
    ecol = lax.broadcasted_iota(jnp.int32, (N_EXPERTS, N_EXPERTS), 1)
    lower = jnp.where(ecol < erow, 1.0, 0.0).astype(BF16)
    eidx = lax.broadcasted_iota(jnp.int32, (N_EXPERTS, sub), 0)
    eidx_f = eidx.astype(F32)

    for s in range(x_ref.shape[0] // sub):
        rows = slice(s * sub, (s + 1) * sub)
        nblk = sub // S5_BLK
        by_pos = jnp.concatenate([r[s * nblk:(s + 1) * nblk, :] for r in ssm_refs], axis=0)
        ssm = jnp.dot(_block_order(sub, inverse=True), by_pos, preferred_element_type=F32)
        y5 = jax.nn.gelu(ssm + d_ref[...] * u_ref[rows, :].astype(F32))
        glu = jnp.dot(y5.astype(BF16), wglu_ref[...], preferred_element_type=F32)
        ys5 = (y5 * _sigmoid(glu)).astype(BF16)
        a = jnp.dot(ysb_ref[rows, :], wsb_ref[...], preferred_element_type=F32)
        b = jnp.dot(ys5, ws5_ref[...], preferred_element_type=F32)
        ga = _sigmoid(gate_ref[rows, :D_MODEL].astype(F32))
        gb = _sigmoid(gate_ref[rows, D_MODEL:].astype(F32))
        merged = (ga * a + gb * b).astype(BF16)
        x1 = x_ref[rows, :] + jnp.dot(merged, wout_ref[...], preferred_element_type=F32)
        x1_ref[rows, :] = x1
        h2 = _rmsnorm(x1, g_ref[...])
        h2_ref[rows, :] = h2

        logits = lax.dot_general(wr_ref[...], h2, (((1,), (1,)), ((), ())),
                                 precision=lax.Precision.HIGHEST,
                                 preferred_element_type=F32) + br_ref[...]
        work = logits
        sel = jnp.zeros(logits.shape, jnp.bool_)
        top = None
        for k in range(TOP_K):
            m = jnp.max(work, axis=0, keepdims=True)
            first = jnp.min(jnp.where(work == m, eidx, N_EXPERTS), axis=0, keepdims=True)
            pick = eidx == first
            sel = jnp.logical_or(sel, pick)
            work = jnp.where(pick, -jnp.inf, work)
            if k == 0:
                top = m
        e = jnp.where(sel, jnp.exp(logits - top), 0.0)
        gate = e / jnp.sum(e, axis=0, keepdims=True)

        mask = jnp.where(sel, 1.0, 0.0)
        pos = jnp.dot(mask.astype(BF16), before, preferred_element_type=F32) + carry_ref[...]
        carry_ref[...] += jnp.sum(mask, axis=1, keepdims=True)

        rank = jnp.dot(lower, mask.astype(BF16), preferred_element_type=F32)
        for k in range(TOP_K):
            ind = jnp.logical_and(sel, rank == float(k))
            pick1 = lambda v: jnp.sum(jnp.where(ind, v, 0.0), axis=0, keepdims=True)
            ek_ref[k:k + 1, rows] = pick1(eidx_f).astype(jnp.int32)
            pk_ref[k:k + 1, rows] = pick1(pos).astype(jnp.int32)
            gk_ref[k:k + 1, rows] = pick1(gate)

    cnt_ref[...] = jnp.broadcast_to(carry_ref[...], cnt_ref.shape)


def _merge(ssm_pieces, ysb, u, gates, x2, d_row, wglu, wsb, ws5, wout, g_ffn, wr_t, br_col):
    t = x2.shape[0]
    tm = TM_MERGE
    const = lambda *shape: pl.BlockSpec(shape, lambda i: (0,) * len(shape))
    return pl.pallas_call(
        _merge_kernel,
        grid=(t // tm,),
        in_specs=[pl.BlockSpec((tm // S5_BLK, S5_WIDTH), lambda i: (i, 0))] * S5_BLK + [
            pl.BlockSpec((tm, SB_WIDTH), lambda i: (i, 0)),
            pl.BlockSpec((tm, S5_WIDTH), lambda i: (i, 0)),
            pl.BlockSpec((tm, 2 * D_MODEL), lambda i: (i, 0)),
            pl.BlockSpec((tm, D_MODEL), lambda i: (i, 0)),
            const(1, S5_WIDTH), const(S5_WIDTH, S5_WIDTH),
            const(SB_WIDTH, D_MODEL), const(S5_WIDTH, D_MODEL), const(D_MODEL, D_MODEL),
            const(1, D_MODEL), const(N_EXPERTS, D_MODEL), const(N_EXPERTS, 1),
        ],
        out_specs=[
            pl.BlockSpec((tm, D_MODEL), lambda i: (i, 0)),
            pl.BlockSpec((tm, D_MODEL), lambda i: (i, 0)),
            pl.BlockSpec((TOP_K, tm), lambda i: (0, i)),
            pl.BlockSpec((TOP_K, tm), lambda i: (0, i)),
            pl.BlockSpec((TOP_K, tm), lambda i: (0, i)),
            pl.BlockSpec((N_EXPERTS, LANES), lambda i: (0, 0)),
        ],
        out_shape=[
            jax.ShapeDtypeStruct((t, D_MODEL), F32),
            jax.ShapeDtypeStruct((t, D_MODEL), F32),
            jax.ShapeDtypeStruct((TOP_K, t), jnp.int32),
            jax.ShapeDtypeStruct((TOP_K, t), jnp.int32),
            jax.ShapeDtypeStruct((TOP_K, t), F32),
            jax.ShapeDtypeStruct((N_EXPERTS, LANES), F32),
        ],
        scratch_shapes=[pltpu.VMEM((N_EXPERTS, 1), F32)],
        compiler_params=pltpu.CompilerParams(
            dimension_semantics=("arbitrary",), vmem_limit_bytes=VMEM_LIMIT),
        name="merge_router",
    )(*ssm_pieces, ysb, u, gates, x2, d_row, wglu, wsb, ws5, wout, g_ffn, wr_t, br_col)


def _layout(ek, pk, t):
    tm = TM_DISPATCH
    ni = t // tm
    ek3, pk3 = ek.reshape(TOP_K, ni, tm), pk.reshape(TOP_K, ni, tm)
    cnt = jnp.stack([jnp.sum(ek3 == e, axis=(0, 2)) for e in range(N_EXPERTS)], axis=1)
    cnt = cnt.astype(jnp.int32)
    seg = (cnt + SUBLANES - 1) // SUBLANES * SUBLANES
    group = jnp.sum(seg, axis=0)
    padded = (group + TR - 1) // TR * TR
    ends = jnp.cumsum(padded)
    dst = (ends - padded)[None, :] + jnp.cumsum(seg, axis=0) - seg
    stage = jnp.cumsum(seg, axis=1) - seg
    base = stage - (jnp.cumsum(cnt, axis=0) - cnt)
    slot = pk3
    for e in range(N_EXPERTS):
        slot = slot + jnp.where(ek3 == e, base[None, :, e, None], 0)
    ntile = (TOP_K * t + ni * N_EXPERTS * (SUBLANES - 1)) // TR + 1 + N_EXPERTS
    nused = ends[-1] // TR
    tile_start = jnp.arange(ntile, dtype=jnp.int32) * TR
    tile_expert = jnp.sum(tile_start[:, None] >= ends[None, :], axis=1).astype(jnp.int32)
    tile_expert = jnp.minimum(tile_expert, jnp.take(tile_expert, nused - 1))
    changes = jnp.concatenate([jnp.zeros((1,), jnp.int32),
                               (tile_expert[1:] != tile_expert[:-1]).astype(jnp.int32)])
    tile_parity = jnp.cumsum(changes) % 2
    tail_tile = jnp.maximum(ends // TR - 1, 0).astype(jnp.int32)
    segs = (stage.reshape(-1), dst.reshape(-1).astype(jnp.int32), seg.reshape(-1))
    return (slot.reshape(TOP_K, t), segs, group.astype(jnp.int32), tail_tile, tile_expert,
            tile_parity.astype(jnp.int32), nused.reshape(1).astype(jnp.int32), ntile)


def _segment_copy(stage_ref, dst_ref, len_ref, idx, staging, buffer, sem, to_buffer):
    aligned = lambda v: pl.multiple_of(v, SUBLANES)
    n = aligned(len_ref[idx])
    a = staging.at[pl.ds(aligned(stage_ref[idx]), n), :]
    b = buffer.at[pl.ds(aligned(dst_ref[idx]), n), :]
    return pltpu.make_async_copy(a, b, sem) if to_buffer else pltpu.make_async_copy(b, a, sem)


def _dispatch_kernel(stage_ref, dst_ref, len_ref, grp_ref, tail_ref, nused_ref, slot_ref, h2_ref,
                     xg_hbm, s_ref, zero_ref, sem, zsem):
    i = pl.program_id(0)

    @pl.when(i == 0)
    def _():
        zero_ref[...] = jnp.zeros_like(zero_ref)
        fill = lambda e: pltpu.make_async_copy(
            zero_ref, xg_hbm.at[pl.ds(pl.multiple_of(tail_ref[e] * TR, TR), TR), :], zsem)
        for e in range(N_EXPERTS):
            @pl.when(grp_ref[e] > 0)
            def _():
                fill(e).start()
        for e in range(N_EXPERTS):
            @pl.when(grp_ref[e] > 0)
            def _():
                fill(e).wait()

        spare = lambda r: pltpu.make_async_copy(
            zero_ref, xg_hbm.at[pl.ds(pl.multiple_of(r * TR, TR), TR), :], zsem)
        ntile = xg_hbm.shape[0] // TR
        lax.fori_loop(nused_ref[0], ntile, lambda r, c: (spare(r).start(), c)[1], 0)
        lax.fori_loop(nused_ref[0], ntile, lambda r, c: (spare(r).wait(), c)[1], 0)

    rowid = lax.broadcasted_iota(jnp.int32, (s_ref.shape[1], h2_ref.shape[0]), 0)
    hit = slot_ref[0:1, :] == rowid
    for k in range(1, TOP_K):
        hit = jnp.logical_or(hit, slot_ref[k:k + 1, :] == rowid)
    onehot = jnp.where(hit, 1.0, 0.0).astype(BF16)
    buf = lax.rem(i, 2)
    s_ref[buf] = jnp.dot(onehot, h2_ref[...].astype(BF16), preferred_element_type=F32)

    def each_segment(tile, buf, action):
        def body(e, carry):
            idx = tile * N_EXPERTS + e

            @pl.when(len_ref[idx] > 0)
            def _():
                action(_segment_copy(stage_ref, dst_ref, len_ref, idx, s_ref.at[buf], xg_hbm,
                                     sem.at[buf], True))
            return carry
        lax.fori_loop(0, N_EXPERTS, body, 0)

    each_segment(i, buf, lambda copy: copy.start())

    @pl.when(i >= 1)
    def _():
        each_segment(i - 1, 1 - buf, lambda copy: copy.wait())

    @pl.when(i == pl.num_programs(0) - 1)
    def _():
        each_segment(i, buf, lambda copy: copy.wait())


def _dispatch(slot, segs, group, tail_tile, nused, h2, nrow):
    t = h2.shape[0]
    tm = TM_DISPATCH
    grid_spec = pltpu.PrefetchScalarGridSpec(
        num_scalar_prefetch=6,
        grid=(t // tm,),
        in_specs=[pl.BlockSpec((TOP_K, tm), lambda i, *_: (0, i)),
                  pl.BlockSpec((tm, D_MODEL), lambda i, *_: (i, 0))],
        out_specs=pl.BlockSpec(memory_space=pl.ANY),
        scratch_shapes=[pltpu.VMEM((2, STAGE_ROWS, D_MODEL), F32), pltpu.VMEM((TR, D_MODEL), F32),
                        pltpu.SemaphoreType.DMA((2,)), pltpu.SemaphoreType.DMA(())],
    )
    return pl.pallas_call(
        _dispatch_kernel,
        grid_spec=grid_spec,
        out_shape=jax.ShapeDtypeStruct((nrow, D_MODEL), F32),
        compiler_params=pltpu.CompilerParams(
            dimension_semantics=("arbitrary",), vmem_limit_bytes=VMEM_LIMIT),
        name="moe_dispatch",
    )(*segs, group, tail_tile, nused, slot, h2)


def _experts_kernel(te_ref, par_ref, nused_ref, xg_ref, wup_ref, bup_ref, wdn_ref, bdn_ref, y_ref,
                    wup_bf, wdn_bf):
    s = pl.program_id(0)
    nused = nused_ref[0]
    r = s - 1

    @pl.when(jnp.logical_and(r >= 0, r < nused))
    def _():
        slot = par_ref[jnp.maximum(r, 0)]
        x = xg_ref[...].astype(BF16)
        hid = jnp.dot(x, wup_bf[slot], preferred_element_type=F32) + bup_ref[0]
        glu = jnp.minimum(hid[:, :D_FF], SWIGLU_LIMIT)
        lin = jnp.clip(hid[:, D_FF:], -SWIGLU_LIMIT, SWIGLU_LIMIT)
        act = glu * _sigmoid(SWIGLU_ALPHA * glu) * (lin + 1.0)
        y_ref[...] = jnp.dot(act.astype(BF16), wdn_bf[slot], preferred_element_type=F32) + bdn_ref[0]

    @pl.when(r >= nused)
    def _():
        y_ref[...] = jnp.zeros_like(y_ref)

    last = te_ref.shape[0] - 1
    cur = jnp.minimum(s, last)
    new_expert = jnp.logical_or(s == 0, te_ref[cur] != te_ref[jnp.maximum(cur - 1, 0)])

    @pl.when(jnp.logical_and(s < nused, new_expert))
    def _():
        slot = par_ref[cur]
        wup_bf[slot] = wup_ref[0].astype(BF16)
        wdn_bf[slot] = wdn_ref[0].astype(BF16)


def _experts(tile_expert, tile_parity, nused, xg, w_up, b_up, w_down, b_down):
    ntile = tile_expert.shape[0]
    ahead = lambda s, te, par, nu: (te[jnp.minimum(s, ntile - 1)], 0, 0)
    current = lambda s, te, par, nu: (te[jnp.maximum(s - 1, 0)], 0, 0)
    by_block = lambda s, te, par, nu: (jnp.maximum(s - 1, 0), 0)
    grid_spec = pltpu.PrefetchScalarGridSpec(
        num_scalar_prefetch=3,
        grid=(ntile + 1,),
        in_specs=[
            pl.BlockSpec((TR, D_MODEL), by_block),
            pl.BlockSpec((1, D_MODEL, 2 * D_FF), ahead),
            pl.BlockSpec((1, 1, 2 * D_FF), current),
            pl.BlockSpec((1, D_FF, D_MODEL), ahead),
            pl.BlockSpec((1, 1, D_MODEL), current),
        ],
        out_specs=pl.BlockSpec((TR, D_MODEL), by_block),
        scratch_shapes=[pltpu.VMEM((2, D_MODEL, 2 * D_FF), BF16),
                        pltpu.VMEM((2, D_FF, D_MODEL), BF16)],
    )
    return pl.pallas_call(
        _experts_kernel,
        grid_spec=grid_spec,
        out_shape=jax.ShapeDtypeStruct(xg.shape, F32),
        compiler_params=pltpu.CompilerParams(
            dimension_semantics=("arbitrary",), vmem_limit_bytes=VMEM_LIMIT),
        name="moe_experts",
    )(tile_expert, tile_parity, nused, xg, w_up, b_up, w_down, b_down)


def _gather_combine_kernel(stage_ref, dst_ref, len_ref, slot_ref, gk_ref, x1_ref, gfin_ref, y_hbm,
                           o_ref, sbuf, sem):
    i = pl.program_id(0)
    nstep = pl.num_programs(0)

    def each_segment(tile, buf, action):
        def body(e, carry):
            idx = tile * N_EXPERTS + e

            @pl.when(len_ref[idx] > 0)
            def _():
                action(_segment_copy(stage_ref, dst_ref, len_ref, idx, sbuf.at[buf], y_hbm,
                                     sem.at[buf], False))
            return carry
        lax.fori_loop(0, N_EXPERTS, body, 0)

    @pl.when(i == 0)
    def _():
        sbuf[...] = jnp.zeros_like(sbuf)
        each_segment(0, 0, lambda copy: copy.start())

    @pl.when(i + 1 < nstep)
    def _():
        each_segment(i + 1, lax.rem(i + 1, 2), lambda copy: copy.start())

    buf = lax.rem(i, 2)
    each_segment(i, buf, lambda copy: copy.wait())

    lane = lax.broadcasted_iota(jnp.int32, (x1_ref.shape[0], sbuf.shape[1]), 1)
    weight = jnp.where(slot_ref[0] == lane, gk_ref[0], 0.0)
    for k in range(1, TOP_K):
        weight = weight + jnp.where(slot_ref[k] == lane, gk_ref[k], 0.0)
    hi = weight.astype(BF16)
    lo = (weight - hi.astype(F32)).astype(BF16)
    rows = sbuf[buf].astype(BF16)
    acc = (x1_ref[...] + jnp.dot(hi, rows, preferred_element_type=F32)
           + jnp.dot(lo, rows, preferred_element_type=F32))
    o_ref[...] = _rmsnorm(acc, gfin_ref[...])


def _gather_combine(slot_col, segs, x1, gk_col, g_fin, y):
    t = x1.shape[0]
    tm = TM_DISPATCH
    grid_spec = pltpu.PrefetchScalarGridSpec(
        num_scalar_prefetch=3,
        grid=(t // tm,),
        in_specs=[
            pl.BlockSpec((TOP_K, tm, 1), lambda i, *_: (0, i, 0)),
            pl.BlockSpec((TOP_K, tm, 1), lambda i, *_: (0, i, 0)),
            pl.BlockSpec((tm, D_MODEL), lambda i, *_: (i, 0)),
            pl.BlockSpec((1, D_MODEL), lambda i, *_: (0, 0)),
            pl.BlockSpec(memory_space=pl.ANY),
        ],
        out_specs=pl.BlockSpec((tm, D_MODEL), lambda i, *_: (i, 0)),
        scratch_shapes=[pltpu.VMEM((2, STAGE_ROWS, D_MODEL), F32), pltpu.SemaphoreType.DMA((2,))],
    )
    return pl.pallas_call(
        _gather_combine_kernel,
        grid_spec=grid_spec,
        out_shape=jax.ShapeDtypeStruct((t, D_MODEL), F32),
        compiler_params=pltpu.CompilerParams(
            dimension_semantics=("arbitrary",), vmem_limit_bytes=VMEM_LIMIT),
        name="gather_combine_norm",
    )(*segs, slot_col, gk_col, x1, g_fin, y)


def kernel(x, norm_mix, w_in, s5_lam_re, s5_lam_im, s5_b_re, s5_b_im, s5_c_re, s5_c_im, s5_d,
           s5_log_step, w_glu, w_branch_sb, w_branch_s5, w_out, norm_ffn, w_router, b_router,
           w_up, b_up, w_down, b_down, norm_final):
    bsz, seq, d = x.shape
    t = bsz * seq
    assert norm_mix.shape[0] == 1, "single-layer block"
    l = 0
    x2 = x.reshape(t, d).astype(F32)
    qkv, u, ublk, gates = _inproj(x2, norm_mix[l].reshape(1, d).astype(F32), w_in[l].astype(BF16))
    ysb = _attention(qkv.reshape(bsz, seq, 3 * SB_WIDTH))
    ssm = _ssm(ublk, seq, *_ssm_params(s5_lam_re[l], s5_lam_im[l], s5_b_re[l], s5_b_im[l],
                                       s5_c_re[l], s5_c_im[l], s5_log_step[l]))
    x1, h2, ek, pk, gk, cnt = _merge(
        ssm, ysb.reshape(t, SB_WIDTH), u, gates, x2,
        s5_d[l].reshape(1, S5_WIDTH).astype(F32), w_glu[l].astype(BF16),
        w_branch_sb[l].astype(BF16), w_branch_s5[l].astype(BF16), w_out[l].astype(BF16),
        norm_ffn[l].reshape(1, d).astype(F32), w_router[l].T.astype(F32),
        b_router[l].reshape(N_EXPERTS, 1).astype(F32))
    slot, segs, group, tail_tile, tile_expert, tile_parity, nused, ntile = _layout(ek, pk, t)
    xg = _dispatch(slot, segs, group, tail_tile, nused, h2, ntile * TR)
    y = _experts(tile_expert, tile_parity, nused, xg, w_up[l],
                 b_up[l].reshape(N_EXPERTS, 1, 2 * D_FF), w_down[l],
                 b_down[l].reshape(N_EXPERTS, 1, D_MODEL))
    out = _gather_combine(slot.reshape(TOP_K, t, 1), segs, x1, gk.reshape(TOP_K, t, 1),
                          norm_final.reshape(1, d).astype(F32), y)
    return out.reshape(bsz, seq, d).astype(x.dtype)
```
